```python
import jax, jax.numpy as jnp
from jax import lax
import numpy as np

D_MODEL = 4096
BATCH = 4
SEQ = 2048
DEPTH = 4
DEC_BATCH = 128
DEC_SEQ = 8
PAST_LEN = 16384
PAGE_SIZE = 128

N_MIXERS = 2
N_GM = (DEPTH + 1) // 2
N_RET = DEPTH // 2
CHUNK = 128
GM_WIDTH = D_MODEL
GM_GROUPS = 16
GM_GROUP_DIM = GM_WIDTH // GM_GROUPS
RET_HEADS = 16
RET_DK = D_MODEL // RET_HEADS
RET_DV = 2 * RET_DK
RET_QK = RET_HEADS * RET_DK
RET_V = RET_HEADS * RET_DV
ROPE_BASE = 10000.0
D_FF = 11008
CONV_WIDTH = 3
EPS = 1e-6

kernel_name = "hybrid_gmlp_retention_convffn_step"


def rms_norm(x, gain):
    xf = x.astype(jnp.float32)
    y = xf * lax.rsqrt(jnp.mean(xf * xf, axis=-1, keepdims=True) + EPS)
    return (y * gain.astype(jnp.float32)).astype(x.dtype)


def rotary(t, pos):
    half = t.shape[-1] // 2
    inv_freq = ROPE_BASE ** (-jnp.arange(half, dtype=jnp.float32) / half)
    ang = pos.astype(jnp.float32)[:, None] * inv_freq[None, :]
    cos = jnp.cos(ang)[None, :, None, :]
    sin = jnp.sin(ang)[None, :, None, :]
    tf = t.astype(jnp.float32)
    t1, t2 = tf[..., :half], tf[..., half:]
    return jnp.concatenate([t1 * cos - t2 * sin, t2 * cos + t1 * sin], axis=-1).astype(t.dtype)


def chunk_gmlp(h, w_in, v_gain, w_s, b_s, w_out):
    bsz, seq_len, _ = h.shape
    lc = min(seq_len, CHUNK)
    nc = seq_len // lc
    z = jax.nn.gelu(h @ w_in)
    u, v = jnp.split(z, 2, axis=-1)
    v = rms_norm(v, v_gain)
    vb = v.reshape(bsz, nc, lc, GM_GROUPS, GM_GROUP_DIM)
    w = jnp.tril(w_s[:, :lc, :lc])
    sv = jnp.einsum('gij,bnjgc->bnigc', w, vb) + b_s[:, :lc].T[None, None, :, :, None]
    out = (u * sv.reshape(bsz, seq_len, GM_WIDTH)) @ w_out
    return out, v


def retention_chunk(s, qkv, log_gamma):
    q, k, v = qkv
    lc = q.shape[2]
    idx = jnp.arange(lc, dtype=jnp.float32)
    lg = log_gamma[:, None]
    diff = idx[:, None] - idx[None, :]
    decay = jnp.where(diff >= 0, jnp.exp(jnp.maximum(diff, 0.0)[None] * lg[..., None]), 0.0)
    scores = jnp.einsum('bhid,bhjd->bhij', q, k) * decay[None]
    o = jnp.einsum('bhij,bhjv->bhiv', scores, v)
    o = o + jnp.einsum('bhid,bhdv->bhiv', q, s) * jnp.exp((idx + 1.0) * lg)[None, :, :, None]
    k_w = k * jnp.exp((lc - 1.0 - idx) * lg)[None, :, :, None]
    s_new = jnp.exp(lc * log_gamma)[None, :, None, None] * s + jnp.einsum('bhjd,bhjv->bhdv', k_w, v)
    return s_new, o


def retention(h, pos, s0, w_in, gn_gain, w_out):
    bsz, seq_len, _ = h.shape
    lc = min(seq_len, CHUNK)
    nc = seq_len // lc
    proj = h @ w_in
    q, k, v, g = jnp.split(proj, [RET_QK, 2 * RET_QK, 2 * RET_QK + RET_V], axis=-1)
    q = rotary(q.reshape(bsz, seq_len, RET_HEADS, RET_DK), pos)
    k = rotary(k.reshape(bsz, seq_len, RET_HEADS, RET_DK), pos) * (RET_DK ** -0.5)
    v = v.reshape(bsz, seq_len, RET_HEADS, RET_DV)

    def to_chunks(t):
        return t.astype(jnp.float32).reshape(bsz, nc, lc, RET_HEADS, -1).transpose(1, 0, 3, 2, 4)

    log_gamma = jnp.log1p(-jnp.exp2(-5.0 - jnp.arange(RET_HEADS, dtype=jnp.float32)))
    s_last, o = lax.scan(lambda s, xs: retention_chunk(s, xs, log_gamma),
                         s0.astype(jnp.float32), (to_chunks(q), to_chunks(k), to_chunks(v)))
    o = o.transpose(1, 0, 3, 2, 4).reshape(bsz, seq_len, RET_HEADS, RET_DV)
    mu = jnp.mean(o, axis=-1, keepdims=True)
    var = jnp.mean(jnp.square(o - mu), axis=-1, keepdims=True)
    o = ((o - mu) * lax.rsqrt(var + EPS)).reshape(bsz, seq_len, RET_V) * gn_gain.astype(jnp.float32)
    out = (o.astype(h.dtype) * jax.nn.silu(g)) @ w_out
    return out, s_last


def conv_ffn(h, conv_prev, w_up, conv_w, conv_b, w_down):
    seq_len = h.shape[1]
    up = h @ w_up
    a, b = jnp.split(up, 2, axis=-1)
    a_pad = jnp.concatenate([conv_prev.astype(a.dtype), a], axis=1)
    conv = conv_b
    for j in range(CONV_WIDTH):
        conv = conv + a_pad[:, j:j + seq_len] * conv_w[j]
    out = (jax.nn.silu(conv) * b) @ w_down
    return out, a_pad[:, -(CONV_WIDTH - 1):]


def decoder_layers(x, c, pos, ret_state, conv_state, w_mod, b_mod, g_norm_mix, g_norm_ffn,
                   g_norm_final, gm_w_in, gm_v_gain, gm_w_s, gm_b_s, gm_w_out, ret_w_in,
                   ret_gn_gain, ret_w_out, ffn_w_up, ffn_conv_w, ffn_conv_b, ffn_w_down):
    c_act = jax.nn.silu(c)
    new_ret, new_v, new_conv = [], [], []
    for i in range(DEPTH):
        mod = (c_act @ w_mod[i] + b_mod[i])[:, None, :]
        sh1, sc1, ga1, sh2, sc2, ga2 = jnp.split(mod, 6, axis=-1)
        h = rms_norm(x, g_norm_mix[i]) * (1.0 + sc1) + sh1
        j = i // N_MIXERS
        if i % N_MIXERS == 0:
            out, v_rows = chunk_gmlp(h, gm_w_in[j], gm_v_gain[j], gm_w_s[j], gm_b_s[j], gm_w_out[j])
            new_v.append(v_rows)
        else:
            out, s_new = retention(h, pos, ret_state[j], ret_w_in[j], ret_gn_gain[j], ret_w_out[j])
            new_ret.append(s_new)
        x = x + ga1 * out
        h = rms_norm(x, g_norm_ffn[i]) * (1.0 + sc2) + sh2
        out, cbuf = conv_ffn(h, conv_state[i], ffn_w_up[i], ffn_conv_w[i], ffn_conv_b[i], ffn_w_down[i])
        new_conv.append(cbuf)
        x = x + ga2 * out
    y = rms_norm(x, g_norm_final)
    return y, jnp.stack(new_ret), jnp.stack(new_v), jnp.stack(new_conv)


def setup_inputs(seed: int = 0) -> dict:
    key = jax.random.key(seed)
    ks = jax.random.split(key, 24)
    f32 = jnp.float32
    nrm = lambda k, shape, s: jax.random.normal(k, shape, f32) * s
    d = D_MODEL
    return {
        "x_prompt": nrm(ks[0], (BATCH, SEQ, d), 1.0),
        "x_sample": nrm(ks[1], (DEC_BATCH, DEC_SEQ, d), 1.0),
        "c_prompt": nrm(ks[2], (BATCH, d), 1.0),
        "c_sample": nrm(ks[3], (DEC_BATCH, d), 1.0),
        "state_ret": nrm(ks[4], (N_RET, DEC_BATCH, RET_HEADS, RET_DK, RET_DV), 0.05),
        "state_conv": nrm(ks[5], (DEPTH, DEC_BATCH, CONV_WIDTH - 1, D_FF), 1.0),
        "w_mod": nrm(ks[6], (DEPTH, d, 6 * d), 0.5 * d ** -0.5),
        "b_mod": nrm(ks[7], (DEPTH, 6 * d), 0.01),
        "g_norm_mix": 1.0 + nrm(ks[8], (DEPTH, d), 0.02),
        "g_norm_ffn": 1.0 + nrm(ks[9], (DEPTH, d), 0.02),
        "g_norm_final": 1.0 + nrm(ks[10], (d,), 0.02),
        "gm_w_in": nrm(ks[11], (N_GM, d, 2 * GM_WIDTH), d ** -0.5),
        "gm_v_gain": 1.0 + nrm(ks[12], (N_GM, GM_WIDTH), 0.02),
        "gm_w_s": nrm(ks[13], (N_GM, GM_GROUPS, CHUNK, CHUNK), 0.5 * CHUNK ** -0.5),
        "gm_b_s": 1.0 + nrm(ks[14], (N_GM, GM_GROUPS, CHUNK), 0.1),
        "gm_w_out": nrm(ks[15], (N_GM, GM_WIDTH, d), GM_WIDTH ** -0.5),
        "ret_w_in": nrm(ks[16], (N_RET, d, 2 * RET_QK + 2 * RET_V), d ** -0.5),
        "ret_gn_gain": 1.0 + nrm(ks[17], (N_RET, RET_V), 0.02),
        "ret_w_out": nrm(ks[18], (N_RET, RET_V, d), RET_V ** -0.5),
        "ffn_w_up": nrm(ks[19], (DEPTH, d, 2 * D_FF), d ** -0.5),
        "ffn_conv_w": nrm(ks[20], (DEPTH, CONV_WIDTH, D_FF), CONV_WIDTH ** -0.5),
        "ffn_conv_b": nrm(ks[21], (DEPTH, D_FF), 0.01),
        "ffn_w_down": nrm(ks[22], (DEPTH, D_FF, d), D_FF ** -0.5),
    }


def reference(x_prompt, x_sample, c_prompt, c_sample, state_ret, state_conv, w_mod, b_mod,
              g_norm_mix, g_norm_ffn, g_norm_final, gm_w_in, gm_v_gain, gm_w_s, gm_b_s, gm_w_out,
              ret_w_in, ret_gn_gain, ret_w_out, ffn_w_up, ffn_conv_w, ffn_conv_b, ffn_w_down):
    weights = (w_mod, b_mod, g_norm_mix, g_norm_ffn, g_norm_final, gm_w_in, gm_v_gain, gm_w_s,
               gm_b_s, gm_w_out, ret_w_in, ret_gn_gain, ret_w_out, ffn_w_up, ffn_conv_w,
               ffn_conv_b, ffn_w_down)
    bsz = x_prompt.shape[0]
    pos_prompt = jnp.arange(x_prompt.shape[1])
    pos_sample = PAST_LEN + jnp.arange(x_sample.shape[1])
    ret0 = jnp.zeros((N_RET, bsz, RET_HEADS, RET_DK, RET_DV), jnp.float32)
    conv0 = jnp.zeros((DEPTH, bsz, CONV_WIDTH - 1, D_FF), x_prompt.dtype)
    y_prompt, ret_p, _, conv_p = decoder_layers(x_prompt, c_prompt, pos_prompt, ret0, conv0, *weights)
    y_sample, ret_s, v_s, conv_s = decoder_layers(x_sample, c_sample, pos_sample, state_ret, state_conv, *weights)
    return (y_prompt, y_sample, ret_p, ret_s, v_s, conv_p, conv_s)
```

```python
import functools

import jax
import jax.numpy as jnp
from jax import lax
from jax.experimental import pallas as pl
from jax.experimental.pallas import tpu as pltpu

D_MODEL = 4096
DEPTH = 4
PAST_LEN = 16384
N_MIXERS = 2
CHUNK = 128
GM_WIDTH = D_MODEL
GM_GROUPS = 16
GM_GROUP_DIM = GM_WIDTH // GM_GROUPS
RET_HEADS = 16
RET_DK = D_MODEL // RET_HEADS
RET_DV = 2 * RET_DK
RET_QK = RET_HEADS * RET_DK
RET_V = RET_HEADS * RET_DV
ROPE_BASE = 10000.0
D_FF = 11008
CONV_WIDTH = 3
EPS = 1e-6

F32 = jnp.float32
BF16 = jnp.bfloat16

SUBLANES = 8
LANES = 128
MIB = 1024 * 1024
VMEM_CAP_MIB = 60
FFN_TF = 256


def _nbytes(shape, dtype):
    n = jnp.dtype(dtype).itemsize
    for s in shape:
        n *= s
    return n


def _params(semantics, pipelined_bytes, temp_bytes=0):
    need = 2 * pipelined_bytes + temp_bytes
    mib = min(VMEM_CAP_MIB, -(-need // MIB) + 4)
    return pltpu.CompilerParams(dimension_semantics=semantics, vmem_limit_bytes=mib * MIB)


class _Path:
    def __init__(self, n_seq, seq_len):
        self.n_seq = n_seq
        self.seq_len = seq_len
        self.rows = n_seq * seq_len
        self.per_row = seq_len < SUBLANES * 2

    def mod_spec(self, tile_rows, layer, which):
        if self.per_row:
            return pl.BlockSpec((None, tile_rows, D_MODEL), lambda i, *_: (layer, i, which))
        tiles_per_seq = self.seq_len // tile_rows
        return pl.BlockSpec((None, None, 1, D_MODEL),
                            lambda i, *_: (layer, i // tiles_per_seq, 0, which))

    def pos_spec(self, tile_rows):
        if self.per_row:
            return pl.BlockSpec((tile_rows, LANES), lambda i, *_: (i, 0))
        tiles_per_seq = self.seq_len // tile_rows
        return pl.BlockSpec((tile_rows, LANES), lambda i, *_: (i % tiles_per_seq, 0))


def _mod_body(c_ref, w_ref, b_ref, o_ref):
    c_act = jax.nn.silu(c_ref[...]).astype(BF16)
    o_ref[...] = jnp.dot(c_act, w_ref[...].astype(BF16), preferred_element_type=F32) + b_ref[...]


def _modulation(c_all, w_mod, b_mod):
    rows = c_all.shape[0]
    n_out = 6 * D_MODEL
    tn = 512
    blocks = _nbytes((rows, D_MODEL), F32) + _nbytes((D_MODEL, tn), F32) + _nbytes((rows, tn), F32)
    return pl.pallas_call(
        _mod_body,
        grid=(DEPTH, n_out // tn),
        in_specs=[pl.BlockSpec((rows, D_MODEL), lambda l, j: (0, 0)),
                  pl.BlockSpec((None, D_MODEL, tn), lambda l, j: (l, 0, j)),
                  pl.BlockSpec((None, 1, tn), lambda l, j: (l, 0, j))],
        out_specs=pl.BlockSpec((None, rows, tn), lambda l, j: (l, 0, j)),
        out_shape=jax.ShapeDtypeStruct((DEPTH, rows, n_out), F32),
        compiler_params=_params(("parallel", "parallel"), blocks, _nbytes((D_MODEL, tn), BF16)),
        name="modulation",
    )(c_all, w_mod, b_mod.reshape(DEPTH, 1, n_out))


def _resid_norm_body(*refs, has_resid, has_mod, emit_x):
    it = iter(refs)
    x_ref = next(it)
    if has_resid:
        y_ref, ga_ref = next(it), next(it)
    gain_ref = next(it)
    if has_mod:
        sc_ref, sh_ref = next(it), next(it)
    if emit_x:
        xo_ref = next(it)
    h_ref = next(it)

    x = x_ref[...]
    if has_resid:
        x = x + ga_ref[...] * y_ref[...]
    if emit_x:
        xo_ref[...] = x
    n = (x * lax.rsqrt(jnp.mean(x * x, axis=-1, keepdims=True) + EPS)) * gain_ref[...]
    if has_mod:
        n = n * (1.0 + sc_ref[...]) + sh_ref[...]
    h_ref[...] = n.astype(h_ref.dtype)


def _resid_norm(path, x, gain, layer, *, mod=None, mod_layer=0, y=None, ga_which=None,
                sc_which=None, sh_which=None, emit_x=False, out_dtype=BF16, name):
    tr = 256
    rows = path.rows
    row_spec = pl.BlockSpec((tr, D_MODEL), lambda i: (i, 0))
    has_resid, has_mod = y is not None, sc_which is not None
    args, specs = [x], [row_spec]
    if has_resid:
        args += [y, mod]
        specs += [row_spec, path.mod_spec(tr, mod_layer, ga_which)]
    args.append(gain)
    specs.append(pl.BlockSpec((None, 1, D_MODEL), lambda i: (layer, 0, 0)))
    if has_mod:
        args += [mod, mod]
        specs += [path.mod_spec(tr, layer, sc_which), path.mod_spec(tr, layer, sh_which)]
    out_shape, out_specs = [], []
    if emit_x:
        out_shape.append(jax.ShapeDtypeStruct((rows, D_MODEL), F32))
        out_specs.append(row_spec)
    out_shape.append(jax.ShapeDtypeStruct((rows, D_MODEL), out_dtype))
    out_specs.append(row_spec)
    blocks = _nbytes((tr, D_MODEL), F32) * (4 + 3 * path.per_row)
    out = pl.pallas_call(
        functools.partial(_resid_norm_body, has_resid=has_resid, has_mod=has_mod, emit_x=emit_x),
        grid=(rows // tr,),
        in_specs=specs,
        out_specs=out_specs,
        out_shape=out_shape,
        compiler_params=_params(("parallel",), blocks, 2 * _nbytes((tr, D_MODEL), F32)),
        name=name,
    )(*args)
    return tuple(out) if emit_x else out[0]


def _mm_body(*refs, kind, scale):
    if kind == "rotary":
        a_ref, w_ref, cos_ref, sin_ref, o_ref = refs
    else:
        a_ref, w_ref, o_ref = refs
    acc = jnp.dot(a_ref[...].astype(BF16), w_ref[...].astype(BF16), preferred_element_type=F32)
    if kind == "gelu":
        o_ref[...] = jax.nn.gelu(acc, approximate=True).astype(o_ref.dtype)
    elif kind == "plain":
        o_ref[...] = acc.astype(o_ref.dtype)
    else:
        cos, sin = cos_ref[...], sin_ref[...]
        half = RET_DK // 2
        for h0 in range(0, acc.shape[1], RET_DK):
            t1 = acc[:, h0:h0 + half]
            t2 = acc[:, h0 + half:h0 + RET_DK]
            o_ref[:, h0:h0 + half] = ((t1 * cos - t2 * sin) * scale).astype(o_ref.dtype)
            o_ref[:, h0 + half:h0 + RET_DK] = ((t2 * cos + t1 * sin) * scale).astype(o_ref.dtype)


def _matmul(path, a, w, layer, *, col0, n_out, kind, out_dtype, cos=None, sin=None, scale=1.0, name):
    rows, k_dim = a.shape
    tm, tn = min(rows, 1024), 512
    joff = col0 // tn
    args = [a, w]
    specs = [pl.BlockSpec((tm, k_dim), lambda i, j: (i, 0)),
             pl.BlockSpec((None, k_dim, tn), lambda i, j: (layer, 0, j + joff))]
    if kind == "rotary":
        args += [cos, sin]
        specs += [path.pos_spec(tm), path.pos_spec(tm)]
    blocks = (_nbytes((tm, k_dim), a.dtype) + _nbytes((k_dim, tn), w.dtype)
              + _nbytes((tm, tn), out_dtype))
    temps = _nbytes((k_dim, tn), BF16) + 2 * _nbytes((tm, tn), F32)
    return pl.pallas_call(
        functools.partial(_mm_body, kind=kind, scale=scale),
        grid=(rows // tm, n_out // tn),
        in_specs=specs,
        out_specs=pl.BlockSpec((tm, tn), lambda i, j: (i, j)),
        out_shape=jax.ShapeDtypeStruct((rows, n_out), out_dtype),
        compiler_params=_params(("parallel", "parallel"), blocks, temps),
        name=name,
    )(*args)


def _mm_kt_body(a_ref, w_ref, o_ref):
    p = jnp.dot(a_ref[...].astype(BF16), w_ref[...].astype(BF16), preferred_element_type=F32)

    @pl.when(pl.program_id(1) == 0)
    def _():
        o_ref[...] = p

    @pl.when(pl.program_id(1) != 0)
    def _():
        o_ref[...] += p


def _matmul_ktiled(a, w, layer, *, name):
    rows, k_dim = a.shape
    n_out = w.shape[-1]
    tm, tk = 512, 512
    blocks = _nbytes((tm, tk), a.dtype) + _nbytes((tk, n_out), w.dtype) + _nbytes((tm, n_out), F32)
    temps = _nbytes((tk, n_out), BF16) + _nbytes((tm, n_out), F32)
    return pl.pallas_call(
        _mm_kt_body,
        grid=(rows // tm, k_dim // tk),
        in_specs=[pl.BlockSpec((tm, tk), lambda i, k: (i, k)),
                  pl.BlockSpec((None, tk, n_out), lambda i, k: (layer, k, 0))],
        out_specs=pl.BlockSpec((tm, n_out), lambda i, k: (i, 0)),
        out_shape=jax.ShapeDtypeStruct((rows, n_out), F32),
        compiler_params=_params(("parallel", "arbitrary"), blocks, temps),
        name=name,
    )(a, w)


def _vnorm_body(v_ref, g_ref, o_ref):
    v = v_ref[...]
    o_ref[...] = (v * lax.rsqrt(jnp.mean(v * v, axis=-1, keepdims=True) + EPS)) * g_ref[...]


def _vnorm(z, v_gain, layer, *, name):
    rows = z.shape[0]
    tr = 256
    blk = _nbytes((tr, GM_WIDTH), F32)
    return pl.pallas_call(
        _vnorm_body,
        grid=(rows // tr,),
        in_specs=[pl.BlockSpec((tr, GM_WIDTH), lambda i: (i, 1)),
                  pl.BlockSpec((None, 1, GM_WIDTH), lambda i: (layer, 0, 0))],
        out_specs=pl.BlockSpec((tr, GM_WIDTH), lambda i: (i, 0)),
        out_shape=jax.ShapeDtypeStruct((rows, GM_WIDTH), F32),
        compiler_params=_params(("parallel",), 2 * blk, 2 * blk),
        name=name,
    )(z, v_gain)


def _gate_body(u_ref, vn_ref, w_ref, b_ref, o_ref, *, span):
    w, bias = w_ref[...], b_ref[...]
    for r0 in range(0, u_ref.shape[0], span):
        rows = slice(r0, r0 + span)
        sv = jnp.dot(w, vn_ref[rows, :].astype(BF16), preferred_element_type=F32) + bias
        o_ref[rows, :] = (u_ref[rows, :] * sv).astype(o_ref.dtype)


def _gate(z, vn, w_mix, bias, *, name):
    rows = z.shape[0]
    span = w_mix.shape[1]
    rb = 1024
    gd = GM_GROUP_DIM
    blocks = (2 * _nbytes((rb, gd), F32) + _nbytes((span, span), BF16)
              + _nbytes((span, LANES), F32) + _nbytes((rb, gd), BF16))
    return pl.pallas_call(
        functools.partial(_gate_body, span=span),
        grid=(rows // rb, GM_GROUPS),
        in_specs=[pl.BlockSpec((rb, gd), lambda c, g: (c, g)),
                  pl.BlockSpec((rb, gd), lambda c, g: (c, g)),
                  pl.BlockSpec((None, span, span), lambda c, g: (g, 0, 0)),
                  pl.BlockSpec((None, span, 1), lambda c, g: (g, 0, 0))],
        out_specs=pl.BlockSpec((rb, gd), lambda c, g: (c, g)),
        out_shape=jax.ShapeDtypeStruct((rows, GM_WIDTH), BF16),
        compiler_params=_params(("parallel", "parallel"), blocks, 4 * _nbytes((rb, gd), F32)),
        name=name,
    )(z, vn, w_mix, bias)


def _ret_tables(lc):
    lg = jnp.log1p(-jnp.exp2(-5.0 - jnp.arange(RET_HEADS, dtype=F32)))
    idx = jnp.arange(lc, dtype=F32)
    diff = idx[:, None] - idx[None, :]
    decay = jnp.where(diff >= 0, jnp.exp(jnp.maximum(diff, 0.0)[None] * lg[:, None, None]), 0.0)
    o_scale = jnp.exp((idx + 1.0)[None, :] * lg[:, None])[..., None]
    k_scale = jnp.exp((lc - 1.0 - idx)[None, :] * lg[:, None])[..., None]
    s_scale = jnp.broadcast_to(jnp.exp(lc * lg)[:, None, None], (RET_HEADS, 1, RET_DV))
    return decay, o_scale, k_scale, s_scale


def _ret_chunk(q, k, v, g, s, decay, o_scale, k_scale, s_scale, gain):
    qb, vb = q.astype(BF16), v.astype(BF16)
    scores = lax.dot_general(qb, k.astype(BF16), (((1,), (1,)), ((), ())),
                             preferred_element_type=F32) * decay
    o = jnp.dot(scores.astype(BF16), vb, preferred_element_type=F32)
    o = o + jnp.dot(qb, s.astype(BF16), preferred_element_type=F32) * o_scale
    k_w = (k * k_scale).astype(BF16)
    s_new = s_scale * s + lax.dot_general(k_w, vb, (((0,), (0,)), ((), ())),
                                          preferred_element_type=F32)
    mu = jnp.mean(o, axis=-1, keepdims=True)
    var = jnp.mean(jnp.square(o - mu), axis=-1, keepdims=True)
    o = ((o - mu) * lax.rsqrt(var + EPS)) * gain
    return o * jax.nn.silu(g), s_new


def _ret_seq_body(q_ref, k_ref, v_ref, g_ref, gain_ref, dec_ref, osc_ref, ksc_ref, ssc_ref, s0_ref,
                  o_ref, s_ref):
    s_ref[...] = s0_ref[...]
    decay, o_scale, k_scale = dec_ref[...], osc_ref[...], ksc_ref[...]
    s_scale, gain = ssc_ref[...], gain_ref[...]

    def chunk(c, carry):
        rows = pl.ds(pl.multiple_of(c * CHUNK, CHUNK), CHUNK)
        out, s_new = _ret_chunk(q_ref[rows, :], k_ref[rows, :], v_ref[rows, :], g_ref[rows, :],
                                s_ref[...], decay, o_scale, k_scale, s_scale, gain)
        s_ref[...] = s_new
        o_ref[rows, :] = out.astype(o_ref.dtype)
        return carry

    lax.fori_loop(0, q_ref.shape[0] // CHUNK, chunk, 0)


def _retention_seq(q, k, v, g, gn_gain, layer, s0, *, name):
    n_seq, seq_len, _ = q.shape
    decay, o_scale, k_scale, s_scale = _ret_tables(CHUNK)
    qk_spec = pl.BlockSpec((None, seq_len, RET_DK), lambda b, h: (b, 0, h))
    v_spec = pl.BlockSpec((None, seq_len, RET_DV), lambda b, h: (b, 0, h))
    s_spec = pl.BlockSpec((None, None, RET_DK, RET_DV), lambda b, h: (b, h, 0, 0))
    s0_spec = pl.BlockSpec((None, None, None, RET_DK, RET_DV), lambda b, h: (layer, b, h, 0, 0))
    tab = lambda shape: pl.BlockSpec((None,) + shape, lambda b, h: (h, 0, 0))
    blocks = (_nbytes((seq_len, RET_DK), q.dtype) + _nbytes((seq_len, RET_DK), k.dtype)
              + _nbytes((seq_len, RET_DV), v.dtype) + _nbytes((seq_len, RET_DV), g.dtype)
              + _nbytes((seq_len, RET_DV), BF16) + 2 * _nbytes((RET_DK, RET_DV), F32))
    return pl.pallas_call(
        _ret_seq_body,
        grid=(n_seq, RET_HEADS),
        in_specs=[qk_spec, qk_spec, v_spec, v_spec,
                  pl.BlockSpec((None, 1, RET_DV), lambda b, h: (layer, 0, h)),
                  tab((CHUNK, CHUNK)), tab((CHUNK, 1)), tab((CHUNK, 1)), tab((1, RET_DV)), s0_spec],
        out_specs=[v_spec, s_spec],
        out_shape=[jax.ShapeDtypeStruct((n_seq, seq_len, RET_V), BF16),
                   jax.ShapeDtypeStruct((n_seq, RET_HEADS, RET_DK, RET_DV), F32)],
        compiler_params=_params(("parallel", "parallel"), blocks, 8 * MIB),
        name=name,
    )(q, k, v, g, gn_gain, decay, o_scale, k_scale, s_scale, s0)


def _ret_step_body(q_ref, k_ref, v_ref, g_ref, gain_ref, dec_ref, osc_ref, ksc_ref, ssc_ref, s0_ref,
                   o_ref, s_ref):
    for h in range(RET_HEADS):
        qk = slice(h * RET_DK, (h + 1) * RET_DK)
        vv = slice(h * RET_DV, (h + 1) * RET_DV)
        out, s_new = _ret_chunk(q_ref[:, qk], k_ref[:, qk], v_ref[:, vv], g_ref[:, vv], s0_ref[h],
                                dec_ref[h], osc_ref[h], ksc_ref[h], ssc_ref[h], gain_ref[:, vv])
        s_ref[h] = s_new
        o_ref[:, vv] = out.astype(o_ref.dtype)


def _retention_step(q, k, v, g, gn_gain, layer, s0, *, name):
    n_seq, seq_len, _ = q.shape
    decay, o_scale, k_scale, s_scale = _ret_tables(seq_len)
    qk_spec = pl.BlockSpec((None, seq_len, RET_QK), lambda b: (b, 0, 0))
    v_spec = pl.BlockSpec((None, seq_len, RET_V), lambda b: (b, 0, 0))
    s_spec = pl.BlockSpec((None, RET_HEADS, RET_DK, RET_DV), lambda b: (b, 0, 0, 0))
    s0_spec = pl.BlockSpec((None, None, RET_HEADS, RET_DK, RET_DV), lambda b: (layer, b, 0, 0, 0))
    whole = lambda a: pl.BlockSpec(a.shape, lambda b: (0,) * a.ndim)
    blocks = 2 * _nbytes((RET_HEADS, RET_DK, RET_DV), F32) + 8 * _nbytes((seq_len, RET_V), F32)
    return pl.pallas_call(
        _ret_step_body,
        grid=(n_seq,),
        in_specs=[qk_spec, qk_spec, v_spec, v_spec,
                  pl.BlockSpec((None, 1, RET_V), lambda b: (layer, 0, 0)),
                  whole(decay), whole(o_scale), whole(k_scale), whole(s_scale), s0_spec],
        out_specs=[v_spec, s_spec],
        out_shape=[jax.ShapeDtypeStruct((n_seq, seq_len, RET_V), F32),
                   jax.ShapeDtypeStruct((n_seq, RET_HEADS, RET_DK, RET_DV), F32)],
        compiler_params=_params(("parallel",), blocks, 8 * MIB),
        name=name,
    )(q, k, v, g, gn_gain, decay, o_scale, k_scale, s_scale, s0)


def _ffn_body(*refs, grouped, tiles_per_seq):
    if grouped:
        h_ref, wa_ref, wb_ref, wd_ref, cw_ref, cb_ref, st_ref, o_ref, tail_ref = refs
    else:
        (h_ref, wa_ref, wb_ref, wd_ref, cw_ref, cb_ref, st_ref, o_ref, tail_ref,
         carry_ref, prev_ref) = refs
    i, j = pl.program_id(0), pl.program_id(1)
    tm = h_ref.shape[0]

    w_up = jnp.concatenate([wa_ref[...].astype(BF16), wb_ref[...].astype(BF16)], axis=1)
    up = jnp.dot(h_ref[...], w_up, preferred_element_type=F32)
    a, b = up[:, :FFN_TF], up[:, FFN_TF:]

    if grouped:
        pos = lax.broadcasted_iota(jnp.int32, a.shape, 0) & (SUBLANES - 1)
        st = st_ref[...]
        a_m1 = jnp.where(pos == 0, pltpu.roll(st, tm - 1, 0), pltpu.roll(a, 1, 0))
        a_m2 = jnp.where(pos < 2, st, pltpu.roll(a, 2, 0))
        tail_ref[...] = a
    else:
        @pl.when(i % tiles_per_seq == 0)
        def _():
            prev_ref[...] = st_ref[...]

        @pl.when(i % tiles_per_seq != 0)
        def _():
            prev_ref[...] = carry_ref[j]

        a_ext = jnp.concatenate([prev_ref[...], a], axis=0)
        a_m1 = pltpu.roll(a_ext, 1, 0)[SUBLANES:]
        a_m2 = pltpu.roll(a_ext, 2, 0)[SUBLANES:]
        last = a[tm - SUBLANES:, :]
        carry_ref[j] = last
        tail_ref[...] = last

    conv = cb_ref[...] + a_m2 * cw_ref[0:1, :] + a_m1 * cw_ref[1:2, :] + a * cw_ref[2:3, :]
    gated = (jax.nn.silu(conv) * b).astype(BF16)
    p = jnp.dot(gated, wd_ref[...].astype(BF16), preferred_element_type=F32)

    @pl.when(j == 0)
    def _():
        o_ref[...] = p

    @pl.when(j != 0)
    def _():
        o_ref[...] += p


def _conv_ffn(path, h, w_up, conv_w, conv_b, w_down, layer, state, *, name):
    rows = path.rows
    tm = 512
    nj = D_FF // FFN_TF
    grouped = path.per_row
    if grouped:
        tiles_per_seq = 0
        st_spec = pl.BlockSpec((None, tm, FFN_TF), lambda i, j: (layer, i, j))
        tail_shape = jax.ShapeDtypeStruct((rows, D_FF), F32)
        tail_spec = pl.BlockSpec((tm, FFN_TF), lambda i, j: (i, j))
        scratch = []
    else:
        tiles_per_seq = path.seq_len // tm
        st_spec = pl.BlockSpec((None, None, SUBLANES, FFN_TF),
                               lambda i, j: (layer, i // tiles_per_seq, 0, j))
        tail_shape = jax.ShapeDtypeStruct((path.n_seq, SUBLANES, D_FF), F32)
        tail_spec = pl.BlockSpec((None, SUBLANES, FFN_TF), lambda i, j: (i // tiles_per_seq, 0, j))
        scratch = [pltpu.VMEM((nj, SUBLANES, FFN_TF), F32), pltpu.VMEM((SUBLANES, FFN_TF), F32)]
    blocks = (_nbytes((tm, D_MODEL), BF16) + 2 * _nbytes((D_MODEL, FFN_TF), w_up.dtype)
              + _nbytes((FFN_TF, D_MODEL), w_down.dtype) + _nbytes((tm, D_MODEL), F32)
              + 2 * _nbytes((tm, FFN_TF), F32))
    temps = (_nbytes((D_MODEL, 2 * FFN_TF), BF16) + _nbytes((FFN_TF, D_MODEL), BF16)
             + 8 * _nbytes((tm, FFN_TF), F32) + _nbytes((tm, D_MODEL), F32))
    return pl.pallas_call(
        functools.partial(_ffn_body, grouped=grouped, tiles_per_seq=tiles_per_seq),
        grid=(rows // tm, nj),
        in_specs=[pl.BlockSpec((tm, D_MODEL), lambda i, j: (i, 0)),
                  pl.BlockSpec((None, D_MODEL, FFN_TF), lambda i, j: (layer, 0, j)),
                  pl.BlockSpec((None, D_MODEL, FFN_TF), lambda i, j: (layer, 0, j + nj)),
                  pl.BlockSpec((None, FFN_TF, D_MODEL), lambda i, j: (layer, j, 0)),
                  pl.BlockSpec((None, CONV_WIDTH, FFN_TF), lambda i, j: (layer, 0, j)),
                  pl.BlockSpec((None, 1, FFN_TF), lambda i, j: (layer, 0, j)),
                  st_spec],
        out_specs=[pl.BlockSpec((tm, D_MODEL), lambda i, j: (i, 0)), tail_spec],
        out_shape=[jax.ShapeDtypeStruct((rows, D_MODEL), F32), tail_shape],
        scratch_shapes=scratch,
        compiler_params=_params(("arbitrary", "arbitrary"), blocks, temps),
        name=name,
    )(h, w_up, w_up, w_down, conv_w, conv_b.reshape(DEPTH, 1, D_FF), state)


def _rotary_tables(pos):
    half = RET_DK // 2
    inv_freq = ROPE_BASE ** (-jnp.arange(half, dtype=F32) / half)
    ang = pos.astype(F32)[:, None] * inv_freq[None, :]
    return jnp.cos(ang), jnp.sin(ang)


def _decoder(path, tag, x, mod, pos, ret_state, conv_state, g_norm_mix, g_norm_ffn, g_norm_final,
             gm_w_in, gm_v_gain, gm_w_s, gm_b_s, gm_w_out, ret_w_in, ret_gn_gain, ret_w_out,
             ffn_w_up, ffn_conv_w, ffn_conv_b, ffn_w_down):
    n_seq, seq_len = path.n_seq, path.seq_len
    lc = min(seq_len, CHUNK)
    seq_dtype = F32 if path.per_row else BF16

    cos, sin = _rotary_tables(pos)
    if path.per_row:
        cos, sin = jnp.tile(cos, (n_seq, 1)), jnp.tile(sin, (n_seq, 1))

    tril = jnp.tril(gm_w_s[:, :, :lc, :lc])
    bias = gm_b_s[:, :, :lc]
    if path.per_row:
        eye = jnp.eye(n_seq, dtype=F32)
        tril = jnp.einsum("ab,ngij->ngaibj", eye, tril).reshape(
            tril.shape[0], GM_GROUPS, path.rows, path.rows)
        bias = jnp.tile(bias, (1, 1, n_seq))
    w_mix = tril.astype(BF16)
    bias = bias[..., None]

    x = x.reshape(path.rows, D_MODEL)
    h = _resid_norm(path, x, g_norm_mix, 0, mod=mod, sc_which=1, sh_which=0, name=f"{tag}_norm_in")
    new_ret, new_v, new_conv = [], [], []
    y_out = None
    for i in range(DEPTH):
        j = i // N_MIXERS
        if i % N_MIXERS == 0:
            z = _matmul(path, h, gm_w_in, j, col0=0, n_out=2 * GM_WIDTH, kind="gelu", out_dtype=F32,
                        name=f"{tag}_gm_in{i}")
            vn = _vnorm(z, gm_v_gain, j, name=f"{tag}_gm_vnorm{i}")
            new_v.append(vn)
            gated = _gate(z, vn, w_mix[j], bias[j], name=f"{tag}_gm_gate{i}")
            y = _matmul(path, gated, gm_w_out, j, col0=0, n_out=D_MODEL, kind="plain", out_dtype=F32,
                        name=f"{tag}_gm_out{i}")
        else:
            proj = functools.partial(_matmul, path, h, ret_w_in, j)
            q = proj(col0=0, n_out=RET_QK, kind="rotary", out_dtype=seq_dtype, cos=cos, sin=sin,
                     name=f"{tag}_ret_q{i}")
            k = proj(col0=RET_QK, n_out=RET_QK, kind="rotary", out_dtype=F32, cos=cos, sin=sin,
                     scale=RET_DK ** -0.5, name=f"{tag}_ret_k{i}")
            v = proj(col0=2 * RET_QK, n_out=RET_V, kind="plain", out_dtype=seq_dtype,
                     name=f"{tag}_ret_v{i}")
            g = proj(col0=2 * RET_QK + RET_V, n_out=RET_V, kind="plain", out_dtype=F32,
                     name=f"{tag}_ret_g{i}")
            shp = lambda t: t.reshape(n_seq, seq_len, t.shape[-1])
            ret = _retention_step if path.per_row else _retention_seq
            o, s_new = ret(shp(q), shp(k), shp(v), shp(g), ret_gn_gain, j, ret_state,
                           name=f"{tag}_retention{i}")
            new_ret.append(s_new)
            y = _matmul_ktiled(o.reshape(path.rows, RET_V), ret_w_out, j, name=f"{tag}_ret_out{i}")

        x, h = _resid_norm(path, x, g_norm_ffn, i, mod=mod, mod_layer=i, y=y, ga_which=2,
                           sc_which=4, sh_which=3, emit_x=True, name=f"{tag}_norm_ffn{i}")
        y, tail = _conv_ffn(path, h, ffn_w_up, ffn_conv_w, ffn_conv_b, ffn_w_down, i, conv_state,
                            name=f"{tag}_ffn{i}")
        if path.per_row:
            tail = tail.reshape(n_seq, seq_len, D_FF)
        new_conv.append(tail[:, -(CONV_WIDTH - 1):, :])
        if i + 1 < DEPTH:
            x, h = _resid_norm(path, x, g_norm_mix, i + 1, mod=mod, mod_layer=i, y=y, ga_which=5,
                               sc_which=1, sh_which=0, emit_x=True, name=f"{tag}_norm_mix{i + 1}")
        else:
            y_out = _resid_norm(path, x, g_norm_final, 0, mod=mod, mod_layer=i, y=y, ga_which=5,
                                out_dtype=F32, name=f"{tag}_norm_out")
    y_out = y_out.reshape(n_seq, seq_len, D_MODEL)
    new_v = [t.reshape(n_seq, seq_len, GM_WIDTH) for t in new_v]
    return y_out, jnp.stack(new_ret), jnp.stack(new_v), jnp.stack(new_conv)


def kernel(x_prompt, x_sample, c_prompt, c_sample, state_ret, state_conv, w_mod, b_mod, g_norm_mix, g_norm_ffn, g_norm_final, gm_w_in, gm_v_gain, gm_w_s, gm_b_s, gm_w_out, ret_w_in, ret_gn_gain, ret_w_out, ffn_w_up, ffn_conv_w, ffn_conv_b, ffn_w_down):
    n_p, len_p, _ = x_prompt.shape
    n_s, len_s, _ = x_sample.shape
    prompt, sample = _Path(n_p, len_p), _Path(n_s, len_s)

    n_c = n_p + n_s
    pad = -n_c % (2 * SUBLANES)
    c_all = jnp.concatenate([c_prompt, c_sample, jnp.zeros((pad, D_MODEL), F32)], axis=0)
    mod_all = _modulation(c_all, w_mod, b_mod)
    mod_p = mod_all[:, :n_p].reshape(DEPTH, n_p, 1, 6 * D_MODEL)
    mod_s = jnp.repeat(mod_all[:, n_p:n_c], len_s, axis=1)

    weights = (g_norm_mix.reshape(DEPTH, 1, D_MODEL), g_norm_ffn.reshape(DEPTH, 1, D_MODEL),
               g_norm_final.reshape(1, 1, D_MODEL), gm_w_in, gm_v_gain.reshape(-1, 1, GM_WIDTH),
               gm_w_s, gm_b_s, gm_w_out, ret_w_in, ret_gn_gain.reshape(-1, 1, RET_V),
               ret_w_out.astype(BF16), ffn_w_up.astype(BF16), ffn_conv_w, ffn_conv_b,
               ffn_w_down.astype(BF16))

    n_ret = state_ret.shape[0]
    hist = CONV_WIDTH - 1
    ret0 = jnp.zeros((n_ret, n_p, RET_HEADS, RET_DK, RET_DV), F32)
    conv_p = jnp.zeros((DEPTH, n_p, SUBLANES, D_FF), F32)
    conv_s = jnp.pad(state_conv, ((0, 0), (0, 0), (0, len_s - hist), (0, 0))).reshape(
        DEPTH, n_s * len_s, D_FF)

    y_p, ret_p, _, cv_p = _decoder(prompt, "p", x_prompt, mod_p, jnp.arange(len_p), ret0, conv_p,
                                   *weights)
    y_s, ret_s, v_s, cv_s = _decoder(sample, "s", x_sample, mod_s, PAST_LEN + jnp.arange(len_s),
                                     state_ret, conv_s, *weights)
    return (y_p, y_s, ret_p, ret_s, v_s, cv_p, cv_s)
```

```python
import functools

import jax
import jax.numpy as jnp
from jax import lax
from jax.experimental import pallas as pl
from jax.experimental.pallas import tpu as pltpu

D_MODEL = 4096
DEPTH = 4
PAST_LEN = 16384
N_MIXERS = 2
CHUNK = 128
GM_WIDTH = D_MODEL
GM_GROUPS = 16
GM_GROUP_DIM = GM_WIDTH // GM_GROUPS
RET_HEADS = 16
RET_DK = D_MODEL // RET_HEADS
RET_DV = 2 * RET_DK
RET_QK = RET_HEADS * RET_DK
RET_V = RET_HEADS * RET_DV
ROPE_BASE = 10000.0
D_FF = 11008
CONV_WIDTH = 3
EPS = 1e-6

F32 = jnp.float32
BF16 = jnp.bfloat16

SUBLANES = 8
LANES = 128
MIB = 1024 * 1024
VMEM_CAP_MIB = 60
FFN_TF = 256
FFN_TM = 1024
FFN_DOWN_CHUNK = 1024


def _nbytes(shape, dtype):
    n = jnp.dtype(dtype).itemsize
    for s in shape:
        n *= s
    return n


def _params(semantics, pipelined_bytes, temp_bytes=0, single_bytes=0):
    need = 2 * pipelined_bytes + single_bytes + temp_bytes
    mib = min(VMEM_CAP_MIB, -(-need // MIB) + 4)
    return pltpu.CompilerParams(dimension_semantics=semantics, vmem_limit_bytes=mib * MIB)


class _Path:
    def __init__(self, n_seq, seq_len):
        self.n_seq = n_seq
        self.seq_len = seq_len
        self.rows = n_seq * seq_len
        self.per_row = seq_len < SUBLANES * 2

    def mod_spec(self, tile_rows, layer, which):
        if self.per_row:
            return pl.BlockSpec((None, tile_rows, D_MODEL), lambda i, *_: (layer, i, which))
        tiles_per_seq = self.seq_len // tile_rows
        return pl.BlockSpec((None, None, 1, D_MODEL),
                            lambda i, *_: (layer, i // tiles_per_seq, 0, which))

    def pos_spec(self, tile_rows):
        if self.per_row:
            return pl.BlockSpec((tile_rows, LANES), lambda i, *_: (i, 0))
        tiles_per_seq = self.seq_len // tile_rows
        return pl.BlockSpec((tile_rows, LANES), lambda i, *_: (i % tiles_per_seq, 0))


def _mod_body(c_ref, w_ref, b_ref, o_ref):
    c_act = jax.nn.silu(c_ref[...]).astype(BF16)
    o_ref[...] = jnp.dot(c_act, w_ref[...].astype(BF16), preferred_element_type=F32) + b_ref[...]


def _modulation(c_all, w_mod, b_mod):
    rows = c_all.shape[0]
    n_out = 6 * D_MODEL
    tn = 512
    blocks = _nbytes((rows, D_MODEL), F32) + _nbytes((D_MODEL, tn), F32) + _nbytes((rows, tn), F32)
    return pl.pallas_call(
        _mod_body,
        grid=(DEPTH, n_out // tn),
        in_specs=[pl.BlockSpec((rows, D_MODEL), lambda l, j: (0, 0)),
                  pl.BlockSpec((None, D_MODEL, tn), lambda l, j: (l, 0, j)),
                  pl.BlockSpec((None, 1, tn), lambda l, j: (l, 0, j))],
        out_specs=pl.BlockSpec((None, rows, tn), lambda l, j: (l, 0, j)),
        out_shape=jax.ShapeDtypeStruct((DEPTH, rows, n_out), F32),
        compiler_params=_params(("parallel", "parallel"), blocks, _nbytes((D_MODEL, tn), BF16)),
        name="modulation",
    )(c_all, w_mod, b_mod.reshape(DEPTH, 1, n_out))


def _resid_norm_body(*refs, has_resid, has_mod, emit_x):
    it = iter(refs)
    x_ref = next(it)
    if has_resid:
        y_ref, ga_ref = next(it), next(it)
    gain_ref = next(it)
    if has_mod:
        sc_ref, sh_ref = next(it), next(it)
    if emit_x:
        xo_ref = next(it)
    h_ref = next(it)

    x = x_ref[...]
    if has_resid:
        x = x + ga_ref[...] * y_ref[...]
    if emit_x:
        xo_ref[...] = x
    n = (x * lax.rsqrt(jnp.mean(x * x, axis=-1, keepdims=True) + EPS)) * gain_ref[...]
    if has_mod:
        n = n * (1.0 + sc_ref[...]) + sh_ref[...]
    h_ref[...] = n.astype(h_ref.dtype)


def _resid_norm(path, x, gain, layer, *, mod=None, mod_layer=0, y=None, ga_which=None,
                sc_which=None, sh_which=None, emit_x=False, out_dtype=BF16, name):
    tr = 256
    rows = path.rows
    row_spec = pl.BlockSpec((tr, D_MODEL), lambda i: (i, 0))
    has_resid, has_mod = y is not None, sc_which is not None
    args, specs = [x], [row_spec]
    if has_resid:
        args += [y, mod]
        specs += [row_spec, path.mod_spec(tr, mod_layer, ga_which)]
    args.append(gain)
    specs.append(pl.BlockSpec((None, 1, D_MODEL), lambda i: (layer, 0, 0)))
    if has_mod:
        args += [mod, mod]
        specs += [path.mod_spec(tr, layer, sc_which), path.mod_spec(tr, layer, sh_which)]
    out_shape, out_specs = [], []
    if emit_x:
        out_shape.append(jax.ShapeDtypeStruct((rows, D_MODEL), F32))
        out_specs.append(row_spec)
    out_shape.append(jax.ShapeDtypeStruct((rows, D_MODEL), out_dtype))
    out_specs.append(row_spec)
    blocks = _nbytes((tr, D_MODEL), F32) * (4 + 3 * path.per_row)
    out = pl.pallas_call(
        functools.partial(_resid_norm_body, has_resid=has_resid, has_mod=has_mod, emit_x=emit_x),
        grid=(rows // tr,),
        in_specs=specs,
        out_specs=out_specs,
        out_shape=out_shape,
        compiler_params=_params(("parallel",), blocks, 2 * _nbytes((tr, D_MODEL), F32)),
        name=name,
    )(*args)
    return tuple(out) if emit_x else out[0]


def _mm_body(*refs, kind, scale):
    if kind == "rotary":
        a_ref, w_ref, cos_ref, sin_ref, o_ref = refs
    else:
        a_ref, w_ref, o_ref = refs
    acc = jnp.dot(a_ref[...].astype(BF16), w_ref[...].astype(BF16), preferred_element_type=F32)
    if kind == "gelu":
        o_ref[...] = jax.nn.gelu(acc, approximate=True).astype(o_ref.dtype)
    elif kind == "plain":
        o_ref[...] = acc.astype(o_ref.dtype)
    else:
        cos, sin = cos_ref[...], sin_ref[...]
        half = RET_DK // 2
        for h0 in range(0, acc.shape[1], RET_DK):
            t1 = acc[:, h0:h0 + half]
            t2 = acc[:, h0 + half:h0 + RET_DK]
            o_ref[:, h0:h0 + half] = ((t1 * cos - t2 * sin) * scale).astype(o_ref.dtype)
            o_ref[:, h0 + half:h0 + RET_DK] = ((t2 * cos + t1 * sin) * scale).astype(o_ref.dtype)


def _matmul(path, a, w, layer, *, col0, n_out, kind, out_dtype, cos=None, sin=None, scale=1.0, name):
    rows, k_dim = a.shape
    tm, tn = min(rows, 1024), 512
    joff = col0 // tn
    args = [a, w]
    specs = [pl.BlockSpec((tm, k_dim), lambda i, j: (i, 0)),
             pl.BlockSpec((None, k_dim, tn), lambda i, j: (layer, 0, j + joff))]
    if kind == "rotary":
        args += [cos, sin]
        specs += [path.pos_spec(tm), path.pos_spec(tm)]
    blocks = (_nbytes((tm, k_dim), a.dtype) + _nbytes((k_dim, tn), w.dtype)
              + _nbytes((tm, tn), out_dtype))
    temps = _nbytes((k_dim, tn), BF16) + 2 * _nbytes((tm, tn), F32)
    return pl.pallas_call(
        functools.partial(_mm_body, kind=kind, scale=scale),
        grid=(rows // tm, n_out // tn),
        in_specs=specs,
        out_specs=pl.BlockSpec((tm, tn), lambda i, j: (i, j)),
        out_shape=jax.ShapeDtypeStruct((rows, n_out), out_dtype),
        compiler_params=_params(("parallel", "parallel"), blocks, temps),
        name=name,
    )(*args)


def _mm_kt_body(a_ref, w_ref, o_ref):
    @pl.when(pl.program_id(1) == 0)
    def _():
        o_ref[...] = jnp.zeros_like(o_ref)

    o_ref[...] += jnp.dot(a_ref[...].astype(BF16), w_ref[...].astype(BF16),
                          preferred_element_type=F32)


def _matmul_ktiled(a, w, layer, *, name):
    rows, k_dim = a.shape
    n_out = w.shape[-1]
    tm, tk = 512, 512
    blocks = _nbytes((tm, tk), a.dtype) + _nbytes((tk, n_out), w.dtype) + _nbytes((tm, n_out), F32)
    temps = _nbytes((tk, n_out), BF16) + _nbytes((tm, n_out), F32)
    return pl.pallas_call(
        _mm_kt_body,
        grid=(rows // tm, k_dim // tk),
        in_specs=[pl.BlockSpec((tm, tk), lambda i, k: (i, k)),
                  pl.BlockSpec((None, tk, n_out), lambda i, k: (layer, k, 0))],
        out_specs=pl.BlockSpec((tm, n_out), lambda i, k: (i, 0)),
        out_shape=jax.ShapeDtypeStruct((rows, n_out), F32),
        compiler_params=_params(("parallel", "arbitrary"), blocks, temps),
        name=name,
    )(a, w)


def _vnorm_body(v_ref, g_ref, o_ref):
    v = v_ref[...]
    o_ref[...] = (v * lax.rsqrt(jnp.mean(v * v, axis=-1, keepdims=True) + EPS)) * g_ref[...]


def _vnorm(z, v_gain, layer, *, name):
    rows = z.shape[0]
    tr = 256
    blk = _nbytes((tr, GM_WIDTH), F32)
    return pl.pallas_call(
        _vnorm_body,
        grid=(rows // tr,),
        in_specs=[pl.BlockSpec((tr, GM_WIDTH), lambda i: (i, 1)),
                  pl.BlockSpec((None, 1, GM_WIDTH), lambda i: (layer, 0, 0))],
        out_specs=pl.BlockSpec((tr, GM_WIDTH), lambda i: (i, 0)),
        out_shape=jax.ShapeDtypeStruct((rows, GM_WIDTH), F32),
        compiler_params=_params(("parallel",), 2 * blk, 2 * blk),
        name=name,
    )(z, v_gain)


def _gate_body(u_ref, vn_ref, w_ref, b_ref, o_ref, *, span):
    w, bias = w_ref[...], b_ref[...]
    for r0 in range(0, u_ref.shape[0], span):
        rows = slice(r0, r0 + span)
        sv = jnp.dot(w, vn_ref[rows, :].astype(BF16), preferred_element_type=F32) + bias
        o_ref[rows, :] = (u_ref[rows, :] * sv).astype(o_ref.dtype)


def _gate(z, vn, w_mix, bias, *, name):
    rows = z.shape[0]
    span = w_mix.shape[1]
    rb = 1024
    gd = GM_GROUP_DIM
    blocks = (2 * _nbytes((rb, gd), F32) + _nbytes((span, span), BF16)
              + _nbytes((span, LANES), F32) + _nbytes((rb, gd), BF16))
    return pl.pallas_call(
        functools.partial(_gate_body, span=span),
        grid=(rows // rb, GM_GROUPS),
        in_specs=[pl.BlockSpec((rb, gd), lambda c, g: (c, g)),
                  pl.BlockSpec((rb, gd), lambda c, g: (c, g)),
                  pl.BlockSpec((None, span, span), lambda c, g: (g, 0, 0)),
                  pl.BlockSpec((None, span, 1), lambda c, g: (g, 0, 0))],
        out_specs=pl.BlockSpec((rb, gd), lambda c, g: (c, g)),
        out_shape=jax.ShapeDtypeStruct((rows, GM_WIDTH), BF16),
        compiler_params=_params(("parallel", "parallel"), blocks, 4 * _nbytes((rb, gd), F32)),
        name=name,
    )(z, vn, w_mix, bias)


def _gate_grouped_body(u_ref, vn_ref, w_ref, b_ref, o_ref):
    vn = vn_ref[...].astype(BF16).astype(F32)
    w = w_ref[...].astype(BF16).astype(F32)
    sv = w[0][None] * vn[:, 0:1, :]
    for j in range(1, w.shape[0]):
        sv = sv + w[j][None] * vn[:, j:j + 1, :]
    o_ref[...] = u_ref[...] * (sv + b_ref[...][None])


def _gate_grouped(z, vn, w_mix_t, bias, *, name):
    n_seq, lc, _ = vn.shape
    gd = GM_GROUP_DIM
    blk = pl.BlockSpec((n_seq, lc, gd), lambda g: (0, 0, g))
    blocks = 3 * _nbytes((n_seq, lc, gd), F32) + (lc + 1) * _nbytes((lc, LANES), F32)
    return pl.pallas_call(
        _gate_grouped_body,
        grid=(GM_GROUPS,),
        in_specs=[blk, blk,
                  pl.BlockSpec((None, lc, lc, 1), lambda g: (g, 0, 0, 0)),
                  pl.BlockSpec((None, lc, 1), lambda g: (g, 0, 0))],
        out_specs=blk,
        out_shape=jax.ShapeDtypeStruct((n_seq, lc, GM_WIDTH), F32),
        compiler_params=_params(("parallel",), blocks, 4 * _nbytes((n_seq, lc, gd), F32)),
        name=name,
    )(z, vn, w_mix_t, bias)


def _ret_tables(lc):
    lg = jnp.log1p(-jnp.exp2(-5.0 - jnp.arange(RET_HEADS, dtype=F32)))
    idx = jnp.arange(lc, dtype=F32)
    diff = idx[:, None] - idx[None, :]
    decay = jnp.where(diff >= 0, jnp.exp(jnp.maximum(diff, 0.0)[None] * lg[:, None, None]), 0.0)
    o_scale = jnp.exp((idx + 1.0)[None, :] * lg[:, None])[..., None]
    k_scale = jnp.exp((lc - 1.0 - idx)[None, :] * lg[:, None])[..., None]
    s_scale = jnp.broadcast_to(jnp.exp(lc * lg)[:, None, None], (RET_HEADS, 1, RET_DV))
    return decay, o_scale, k_scale, s_scale


def _ret_chunk(q, k, v, g, s, decay, o_scale, k_scale, s_scale, gain):
    qb, vb = q.astype(BF16), v.astype(BF16)
    scores = lax.dot_general(qb, k.astype(BF16), (((1,), (1,)), ((), ())),
                             preferred_element_type=F32) * decay
    o = jnp.dot(scores.astype(BF16), vb, preferred_element_type=F32)
    o = o + jnp.dot(qb, s.astype(BF16), preferred_element_type=F32) * o_scale
    k_w = (k * k_scale).astype(BF16)
    s_new = s_scale * s + lax.dot_general(k_w, vb, (((0,), (0,)), ((), ())),
                                          preferred_element_type=F32)
    mu = jnp.mean(o, axis=-1, keepdims=True)
    var = jnp.mean(jnp.square(o - mu), axis=-1, keepdims=True)
    o = ((o - mu) * lax.rsqrt(var + EPS)) * gain
    return o * jax.nn.silu(g), s_new


def _state_out(n_layers, shape, s_all):
    out_shape = jax.ShapeDtypeStruct((n_layers,) + shape, F32)
    if s_all is None:
        return out_shape, [], []
    return out_shape, [s_all], [pl.BlockSpec(memory_space=pl.ANY)]


def _ret_seq_body(q_ref, k_ref, v_ref, g_ref, gain_ref, dec_ref, osc_ref, ksc_ref, ssc_ref, s0_ref,
                  *rest):
    o_ref, s_ref = rest[-2:]
    s_ref[...] = s0_ref[...]
    decay, o_scale, k_scale = dec_ref[...], osc_ref[...], ksc_ref[...]
    s_scale, gain = ssc_ref[...], gain_ref[...]

    def chunk(c, carry):
        rows = pl.ds(pl.multiple_of(c * CHUNK, CHUNK), CHUNK)
        out, s_new = _ret_chunk(q_ref[rows, :], k_ref[rows, :], v_ref[rows, :], g_ref[rows, :],
                                s_ref[...], decay, o_scale, k_scale, s_scale, gain)
        s_ref[...] = s_new
        o_ref[rows, :] = out.astype(o_ref.dtype)
        return carry

    lax.fori_loop(0, q_ref.shape[0] // CHUNK, chunk, 0)


def _retention_seq(q, k, v, g, gn_gain, layer, s0, s_all, *, name):
    n_seq, seq_len, _ = q.shape
    decay, o_scale, k_scale, s_scale = _ret_tables(CHUNK)
    qk_spec = pl.BlockSpec((None, seq_len, RET_DK), lambda b, h: (b, 0, h))
    v_spec = pl.BlockSpec((None, seq_len, RET_DV), lambda b, h: (b, 0, h))
    s_spec = pl.BlockSpec((None, None, None, RET_DK, RET_DV), lambda b, h: (layer, b, h, 0, 0))
    tab = lambda shape: pl.BlockSpec((None,) + shape, lambda b, h: (h, 0, 0))
    blocks = (_nbytes((seq_len, RET_DK), q.dtype) + _nbytes((seq_len, RET_DK), k.dtype)
              + _nbytes((seq_len, RET_DV), v.dtype) + _nbytes((seq_len, RET_DV), g.dtype)
              + _nbytes((seq_len, RET_DV), BF16) + 2 * _nbytes((RET_DK, RET_DV), F32))
    s_shape, extra_args, extra_specs = _state_out(s0.shape[0], s0.shape[1:], s_all)
    args = [q, k, v, g, gn_gain, decay, o_scale, k_scale, s_scale, s0]
    return pl.pallas_call(
        _ret_seq_body,
        grid=(n_seq, RET_HEADS),
        in_specs=[qk_spec, qk_spec, v_spec, v_spec,
                  pl.BlockSpec((None, 1, RET_DV), lambda b, h: (layer, 0, h)),
                  tab((CHUNK, CHUNK)), tab((CHUNK, 1)), tab((CHUNK, 1)), tab((1, RET_DV)), s_spec]
        + extra_specs,
        out_specs=[v_spec, s_spec],
        out_shape=[jax.ShapeDtypeStruct((n_seq, seq_len, RET_V), BF16), s_shape],
        input_output_aliases={len(args): 1} if extra_args else {},
        compiler_params=_params(("parallel", "parallel"), blocks, 8 * MIB),
        name=name,
    )(*args, *extra_args)


def _ret_step_body(q_ref, k_ref, v_ref, g_ref, gain_ref, dec_ref, osc_ref, ksc_ref, ssc_ref, s0_ref,
                   *rest):
    o_ref, s_ref = rest[-2:]
    for h in range(RET_HEADS):
        qk = slice(h * RET_DK, (h + 1) * RET_DK)
        vv = slice(h * RET_DV, (h + 1) * RET_DV)
        out, s_new = _ret_chunk(q_ref[:, qk], k_ref[:, qk], v_ref[:, vv], g_ref[:, vv], s0_ref[h],
                                dec_ref[h], osc_ref[h], ksc_ref[h], ssc_ref[h], gain_ref[:, vv])
        s_ref[h] = s_new
        o_ref[:, vv] = out.astype(o_ref.dtype)


def _retention_step(q, k, v, g, gn_gain, layer, s0, s_all, *, name):
    n_seq, seq_len, _ = q.shape
    decay, o_scale, k_scale, s_scale = _ret_tables(seq_len)
    qk_spec = pl.BlockSpec((None, seq_len, RET_QK), lambda b: (b, 0, 0))
    v_spec = pl.BlockSpec((None, seq_len, RET_V), lambda b: (b, 0, 0))
    s_spec = pl.BlockSpec((None, None, RET_HEADS, RET_DK, RET_DV), lambda b: (layer, b, 0, 0, 0))
    whole = lambda a: pl.BlockSpec(a.shape, lambda b: (0,) * a.ndim)
    blocks = 2 * _nbytes((RET_HEADS, RET_DK, RET_DV), F32) + 8 * _nbytes((seq_len, RET_V), F32)
    s_shape, extra_args, extra_specs = _state_out(s0.shape[0], s0.shape[1:], s_all)
    args = [q, k, v, g, gn_gain, decay, o_scale, k_scale, s_scale, s0]
    return pl.pallas_call(
        _ret_step_body,
        grid=(n_seq,),
        in_specs=[qk_spec, qk_spec, v_spec, v_spec,
                  pl.BlockSpec((None, 1, RET_V), lambda b: (layer, 0, 0)),
                  whole(decay), whole(o_scale), whole(k_scale), whole(s_scale), s_spec]
        + extra_specs,
        out_specs=[v_spec, s_spec],
        out_shape=[jax.ShapeDtypeStruct((n_seq, seq_len, RET_V), F32), s_shape],
        input_output_aliases={len(args): 1} if extra_args else {},
        compiler_params=_params(("parallel",), blocks, 8 * MIB),
        name=name,
    )(*args, *extra_args)


def _ffn_body(*refs, grouped, tiles_per_seq):
    if grouped:
        h_ref, wa_ref, wb_ref, wd_ref, cw_ref, cb_ref, st_ref, o_ref, tail_ref = refs
    else:
        (h_ref, wa_ref, wb_ref, wd_ref, cw_ref, cb_ref, st_ref, o_ref, tail_ref,
         carry_ref, prev_ref) = refs
    i, j = pl.program_id(0), pl.program_id(1)
    tm = h_ref.shape[0]

    @pl.when(j == 0)
    def _():
        o_ref[...] = jnp.zeros_like(o_ref)

    h = h_ref[...]
    a = jnp.dot(h, wa_ref[...].astype(BF16), preferred_element_type=F32)
    b = jnp.dot(h, wb_ref[...].astype(BF16), preferred_element_type=F32)

    if grouped:
        pos = lax.broadcasted_iota(jnp.int32, a.shape, 0) & (SUBLANES - 1)
        st = st_ref[...]
        a_m1 = jnp.where(pos == 0, pltpu.roll(st, tm - 1, 0), pltpu.roll(a, 1, 0))
        a_m2 = jnp.where(pos < 2, st, pltpu.roll(a, 2, 0))
        tail_ref[...] = a
    else:
        @pl.when(i % tiles_per_seq == 0)
        def _():
            prev_ref[...] = st_ref[...]

        @pl.when(i % tiles_per_seq != 0)
        def _():
            prev_ref[...] = carry_ref[j]

        a_ext = jnp.concatenate([prev_ref[...], a], axis=0)
        a_m1 = pltpu.roll(a_ext, 1, 0)[SUBLANES:]
        a_m2 = pltpu.roll(a_ext, 2, 0)[SUBLANES:]
        last = a[tm - SUBLANES:, :]
        carry_ref[j] = last
        tail_ref[...] = last

    conv = cb_ref[...] + a_m2 * cw_ref[0:1, :] + a_m1 * cw_ref[1:2, :] + a * cw_ref[2:3, :]
    gated = (jax.nn.silu(conv) * b).astype(BF16)
    w_down = wd_ref[...].astype(BF16)
    for c0 in range(0, D_MODEL, FFN_DOWN_CHUNK):
        cols = slice(c0, c0 + FFN_DOWN_CHUNK)
        o_ref[:, cols] += jnp.dot(gated, w_down[:, cols], preferred_element_type=F32)


def _conv_ffn(path, h, w_up, conv_w, conv_b, w_down, layer, state, *, name):
    rows = path.rows
    tm = FFN_TM
    n_tiles, nj = rows // tm, D_FF // FFN_TF
    grouped = path.per_row
    if grouped:
        tiles_per_seq = 0
        st_spec = pl.BlockSpec((None, tm, FFN_TF), lambda i, j: (layer, i, j))
        tail_shape = jax.ShapeDtypeStruct((rows, D_FF), F32)
        tail_spec = pl.BlockSpec((tm, FFN_TF), lambda i, j: (i, j))
        scratch = []
    else:
        tiles_per_seq = path.seq_len // tm
        st_spec = pl.BlockSpec((None, None, SUBLANES, FFN_TF),
                               lambda i, j: (layer, i // tiles_per_seq, 0, j))
        tail_shape = jax.ShapeDtypeStruct((n_tiles, SUBLANES, D_FF), F32)
        tail_spec = pl.BlockSpec((None, SUBLANES, FFN_TF), lambda i, j: (i, 0, j))
        scratch = [pltpu.VMEM((nj, SUBLANES, FFN_TF), F32), pltpu.VMEM((SUBLANES, FFN_TF), F32)]
    once = dict(pipeline_mode=pl.Buffered(1))
    single = _nbytes((tm, D_MODEL), BF16) + _nbytes((tm, D_MODEL), F32)
    blocks = (2 * _nbytes((D_MODEL, FFN_TF), w_up.dtype) + _nbytes((FFN_TF, D_MODEL), w_down.dtype)
              + 2 * _nbytes((tm, FFN_TF), F32))
    temps = 10 * _nbytes((tm, FFN_TF), F32) + _nbytes((tm, FFN_DOWN_CHUNK), F32)
    out, tail = pl.pallas_call(
        functools.partial(_ffn_body, grouped=grouped, tiles_per_seq=tiles_per_seq),
        grid=(n_tiles, nj),
        in_specs=[pl.BlockSpec((tm, D_MODEL), lambda i, j: (i, 0), **once),
                  pl.BlockSpec((None, D_MODEL, FFN_TF), lambda i, j: (layer, 0, j)),
                  pl.BlockSpec((None, D_MODEL, FFN_TF), lambda i, j: (layer, 0, j + nj)),
                  pl.BlockSpec((None, FFN_TF, D_MODEL), lambda i, j: (layer, j, 0)),
                  pl.BlockSpec((None, CONV_WIDTH, FFN_TF), lambda i, j: (layer, 0, j)),
                  pl.BlockSpec((None, 1, FFN_TF), lambda i, j: (layer, 0, j)),
                  st_spec],
        out_specs=[pl.BlockSpec((tm, D_MODEL), lambda i, j: (i, 0), **once), tail_spec],
        out_shape=[jax.ShapeDtypeStruct((rows, D_MODEL), F32), tail_shape],
        scratch_shapes=scratch,
        compiler_params=_params(("arbitrary", "arbitrary"), blocks, temps, single),
        name=name,
    )(h, w_up, w_up, w_down, conv_w, conv_b.reshape(DEPTH, 1, D_FF), state)
    if not grouped:
        tail = tail[tiles_per_seq - 1::tiles_per_seq]
    return out, tail


def _rotary_tables(pos):
    half = RET_DK // 2
    inv_freq = ROPE_BASE ** (-jnp.arange(half, dtype=F32) / half)
    ang = pos.astype(F32)[:, None] * inv_freq[None, :]
    return jnp.cos(ang), jnp.sin(ang)


def _decoder(path, tag, x, mod, pos, ret_state, conv_state, g_norm_mix, g_norm_ffn, g_norm_final,
             gm_w_in, gm_v_gain, gm_w_s, gm_b_s, gm_w_out, ret_w_in, ret_gn_gain, ret_w_out,
             ffn_w_up, ffn_conv_w, ffn_conv_b, ffn_w_down):
    n_seq, seq_len = path.n_seq, path.seq_len
    lc = min(seq_len, CHUNK)
    seq_dtype = F32 if path.per_row else BF16

    cos, sin = _rotary_tables(pos)
    if path.per_row:
        cos, sin = jnp.tile(cos, (n_seq, 1)), jnp.tile(sin, (n_seq, 1))

    tril = jnp.tril(gm_w_s[:, :, :lc, :lc])
    bias = gm_b_s[:, :, :lc, None]
    if path.per_row:
        w_mix = jnp.swapaxes(tril, 2, 3)[..., None]
    else:
        w_mix = tril.astype(BF16)

    x = x.reshape(path.rows, D_MODEL)
    h = _resid_norm(path, x, g_norm_mix, 0, mod=mod, sc_which=1, sh_which=0, name=f"{tag}_norm_in")
    new_v, new_conv = [], []
    s_all = None
    y_out = None
    for i in range(DEPTH):
        j = i // N_MIXERS
        if i % N_MIXERS == 0:
            z = _matmul(path, h, gm_w_in, j, col0=0, n_out=2 * GM_WIDTH, kind="gelu", out_dtype=F32,
                        name=f"{tag}_gm_in{i}")
            vn = _vnorm(z, gm_v_gain, j, name=f"{tag}_gm_vnorm{i}")
            new_v.append(vn)
            if path.per_row:
                gated = _gate_grouped(z.reshape(n_seq, seq_len, -1), vn.reshape(n_seq, seq_len, -1),
                                      w_mix[j], bias[j], name=f"{tag}_gm_gate{i}")
                gated = gated.reshape(path.rows, GM_WIDTH)
            else:
                gated = _gate(z, vn, w_mix[j], bias[j], name=f"{tag}_gm_gate{i}")
            y = _matmul(path, gated, gm_w_out, j, col0=0, n_out=D_MODEL, kind="plain", out_dtype=F32,
                        name=f"{tag}_gm_out{i}")
        else:
            proj = functools.partial(_matmul, path, h, ret_w_in, j)
            q = proj(col0=0, n_out=RET_QK, kind="rotary", out_dtype=seq_dtype, cos=cos, sin=sin,
                     name=f"{tag}_ret_q{i}")
            k = proj(col0=RET_QK, n_out=RET_QK, kind="rotary", out_dtype=F32, cos=cos, sin=sin,
                     scale=RET_DK ** -0.5, name=f"{tag}_ret_k{i}")
            v = proj(col0=2 * RET_QK, n_out=RET_V, kind="plain", out_dtype=seq_dtype,
                     name=f"{tag}_ret_v{i}")
            g = proj(col0=2 * RET_QK + RET_V, n_out=RET_V, kind="plain", out_dtype=F32,
                     name=f"{tag}_ret_g{i}")
            shp = lambda t: t.reshape(n_seq, seq_len, t.shape[-1])
            ret = _retention_step if path.per_row else _retention_seq
            o, s_all = ret(shp(q), shp(k), shp(v), shp(g), ret_gn_gain, j, ret_state, s_all,
                           name=f"{tag}_retention{i}")
            o = o.reshape(path.rows, RET_V)
            if path.per_row:
                y = _matmul_ktiled(o, ret_w_out, j, name=f"{tag}_ret_out{i}")
            else:
                y = _matmul(path, o, ret_w_out, j, col0=0, n_out=D_MODEL, kind="plain",
                            out_dtype=F32, name=f"{tag}_ret_out{i}")

        x, h = _resid_norm(path, x, g_norm_ffn, i, mod=mod, mod_layer=i, y=y, ga_which=2,
                           sc_which=4, sh_which=3, emit_x=True, name=f"{tag}_norm_ffn{i}")
        y, tail = _conv_ffn(path, h, ffn_w_up, ffn_conv_w, ffn_conv_b, ffn_w_down, i, conv_state,
                            name=f"{tag}_ffn{i}")
        if path.per_row:
            tail = tail.reshape(n_seq, seq_len, D_FF)
        new_conv.append(tail[:, -(CONV_WIDTH - 1):, :])
        if i + 1 < DEPTH:
            x, h = _resid_norm(path, x, g_norm_mix, i + 1, mod=mod, mod_layer=i, y=y, ga_which=5,
                               sc_which=1, sh_which=0, emit_x=True, name=f"{tag}_norm_mix{i + 1}")
        else:
            y_out = _resid_norm(path, x, g_norm_final, 0, mod=mod, mod_layer=i, y=y, ga_which=5,
                                out_dtype=F32, name=f"{tag}_norm_out")
    y_out = y_out.reshape(n_seq, seq_len, D_MODEL)
    new_v = [t.reshape(n_seq, seq_len, GM_WIDTH) for t in new_v]
    return y_out, s_all, jnp.stack(new_v), jnp.stack(new_conv)


def kernel(x_prompt, x_sample, c_prompt, c_sample, state_ret, state_conv, w_mod, b_mod, g_norm_mix, g_norm_ffn, g_norm_final, gm_w_in, gm_v_gain, gm_w_s, gm_b_s, gm_w_out, ret_w_in, ret_gn_gain, ret_w_out, ffn_w_up, ffn_conv_w, ffn_conv_b, ffn_w_down):
    n_p, len_p, _ = x_prompt.shape
    n_s, len_s, _ = x_sample.shape
    prompt, sample = _Path(n_p, len_p), _Path(n_s, len_s)

    n_c = n_p + n_s
    pad = -n_c % (2 * SUBLANES)
    c_all = jnp.concatenate([c_prompt, c_sample, jnp.zeros((pad, D_MODEL), F32)], axis=0)
    mod_all = _modulation(c_all, w_mod, b_mod)
    mod_p = mod_all[:, :n_p].reshape(DEPTH, n_p, 1, 6 * D_MODEL)
    mod_s = jnp.repeat(mod_all[:, n_p:n_c], len_s, axis=1)

    weights = (g_norm_mix.reshape(DEPTH, 1, D_MODEL), g_norm_ffn.reshape(DEPTH, 1, D_MODEL),
               g_norm_final.reshape(1, 1, D_MODEL), gm_w_in, gm_v_gain.reshape(-1, 1, GM_WIDTH),
               gm_w_s, gm_b_s, gm_w_out, ret_w_in, ret_gn_gain.reshape(-1, 1, RET_V),
               ret_w_out.astype(BF16), ffn_w_up.astype(BF16), ffn_conv_w, ffn_conv_b,
               ffn_w_down.astype(BF16))

    n_ret = state_ret.shape[0]
    hist = CONV_WIDTH - 1
    ret0 = jnp.zeros((n_ret, n_p, RET_HEADS, RET_DK, RET_DV), F32)
    conv_p = jnp.zeros((DEPTH, n_p, SUBLANES, D_FF), F32)
    conv_s = jnp.pad(state_conv, ((0, 0), (0, 0), (0, len_s - hist), (0, 0))).reshape(
        DEPTH, n_s * len_s, D_FF)

    y_p, ret_p, _, cv_p = _decoder(prompt, "p", x_prompt, mod_p, jnp.arange(len_p), ret0, conv_p,
                                   *weights)
    y_s, ret_s, v_s, cv_s = _decoder(sample, "s", x_sample, mod_s, PAST_LEN + jnp.arange(len_s),
                                     state_ret, conv_s, *weights)
    return (y_p, y_s, ret_p, ret_s, v_s, cv_p, cv_s)
```

```python
import functools

import jax
import jax.numpy as jnp
from jax import lax
from jax.experimental import pallas as pl
from jax.experimental.pallas import tpu as pltpu

D_MODEL = 4096
DEPTH = 4
PAST_LEN = 16384
N_MIXERS = 2
CHUNK = 128
GM_WIDTH = D_MODEL
GM_GROUPS = 16
GM_GROUP_DIM = GM_WIDTH // GM_GROUPS
RET_HEADS = 16
RET_DK = D_MODEL // RET_HEADS
RET_DV = 2 * RET_DK
RET_QK = RET_HEADS * RET_DK
RET_V = RET_HEADS * RET_DV
ROPE_BASE = 10000.0
D_FF = 11008
CONV_WIDTH = 3
EPS = 1e-6

F32 = jnp.float32
BF16 = jnp.bfloat16

SUBLANES = 8
LANES = 128
MIB = 1024 * 1024
VMEM_CAP_MIB = 60
FFN_TF = 256
FFN_TM = 1024
FFN_DOWN_CHUNK = 1024


def _nbytes(shape, dtype):
    n = jnp.dtype(dtype).itemsize
    for s in shape:
        n *= s
    return n


def _params(semantics, pipelined_bytes, temp_bytes=0, single_bytes=0):
    need = 2 * pipelined_bytes + single_bytes + temp_bytes
    mib = min(VMEM_CAP_MIB, -(-need // MIB) + 4)
    return pltpu.CompilerParams(dimension_semantics=semantics, vmem_limit_bytes=mib * MIB)


class _Path:
    def __init__(self, n_seq, seq_len):
        self.n_seq = n_seq
        self.seq_len = seq_len
        self.rows = n_seq * seq_len
        self.per_row = seq_len < SUBLANES * 2

    def mod_spec(self, tile_rows, layer, which, width=D_MODEL):
        per_vec = D_MODEL // width
        col = (lambda rest: which * per_vec + rest[0]) if per_vec > 1 else (lambda rest: which)
        if self.per_row:
            return pl.BlockSpec((None, tile_rows, width), lambda i, *rest: (layer, i, col(rest)))
        tiles_per_seq = self.seq_len // tile_rows
        return pl.BlockSpec((None, None, 1, width),
                            lambda i, *rest: (layer, i // tiles_per_seq, 0, col(rest)))

    def pos_spec(self, tile_rows):
        if self.per_row:
            return pl.BlockSpec((tile_rows, LANES), lambda i, *_: (i, 0))
        tiles_per_seq = self.seq_len // tile_rows
        return pl.BlockSpec((tile_rows, LANES), lambda i, *_: (i % tiles_per_seq, 0))


def _mod_body(c_ref, w_ref, b_ref, o_ref):
    c_act = jax.nn.silu(c_ref[...]).astype(BF16)
    o_ref[...] = jnp.dot(c_act, w_ref[...].astype(BF16), preferred_element_type=F32) + b_ref[...]


def _modulation(c_all, w_mod, b_mod):
    rows = c_all.shape[0]
    n_out = 6 * D_MODEL
    tn = 512
    blocks = _nbytes((rows, D_MODEL), F32) + _nbytes((D_MODEL, tn), F32) + _nbytes((rows, tn), F32)
    return pl.pallas_call(
        _mod_body,
        grid=(DEPTH, n_out // tn),
        in_specs=[pl.BlockSpec((rows, D_MODEL), lambda l, j: (0, 0)),
                  pl.BlockSpec((None, D_MODEL, tn), lambda l, j: (l, 0, j)),
                  pl.BlockSpec((None, 1, tn), lambda l, j: (l, 0, j))],
        out_specs=pl.BlockSpec((None, rows, tn), lambda l, j: (l, 0, j)),
        out_shape=jax.ShapeDtypeStruct((DEPTH, rows, n_out), F32),
        compiler_params=_params(("parallel", "parallel"), blocks, _nbytes((D_MODEL, tn), BF16)),
        name="modulation",
    )(c_all, w_mod, b_mod.reshape(DEPTH, 1, n_out))


def _resid_norm_body(*refs, has_resid, has_mod, emit_x):
    it = iter(refs)
    x_ref = next(it)
    if has_resid:
        y_ref, ga_ref = next(it), next(it)
    gain_ref = next(it)
    if has_mod:
        sc_ref, sh_ref = next(it), next(it)
    if emit_x:
        xo_ref = next(it)
    h_ref = next(it)

    x = x_ref[...]
    if has_resid:
        x = x + ga_ref[...] * y_ref[...]
    if emit_x:
        xo_ref[...] = x
    n = (x * lax.rsqrt(jnp.mean(x * x, axis=-1, keepdims=True) + EPS)) * gain_ref[...]
    if has_mod:
        n = n * (1.0 + sc_ref[...]) + sh_ref[...]
    h_ref[...] = n.astype(h_ref.dtype)


def _resid_norm(path, x, gain, layer, *, mod=None, mod_layer=0, y=None, ga_which=None,
                sc_which=None, sh_which=None, emit_x=False, out_dtype=BF16, name):
    tr = 256
    rows = path.rows
    row_spec = pl.BlockSpec((tr, D_MODEL), lambda i: (i, 0))
    has_resid, has_mod = y is not None, sc_which is not None
    args, specs = [x], [row_spec]
    if has_resid:
        args += [y, mod]
        specs += [row_spec, path.mod_spec(tr, mod_layer, ga_which)]
    args.append(gain)
    specs.append(pl.BlockSpec((None, 1, D_MODEL), lambda i: (layer, 0, 0)))
    if has_mod:
        args += [mod, mod]
        specs += [path.mod_spec(tr, layer, sc_which), path.mod_spec(tr, layer, sh_which)]
    out_shape, out_specs = [], []
    if emit_x:
        out_shape.append(jax.ShapeDtypeStruct((rows, D_MODEL), F32))
        out_specs.append(row_spec)
    out_shape.append(jax.ShapeDtypeStruct((rows, D_MODEL), out_dtype))
    out_specs.append(row_spec)
    blocks = _nbytes((tr, D_MODEL), F32) * (4 + 3 * path.per_row)
    out = pl.pallas_call(
        functools.partial(_resid_norm_body, has_resid=has_resid, has_mod=has_mod, emit_x=emit_x),
        grid=(rows // tr,),
        in_specs=specs,
        out_specs=out_specs,
        out_shape=out_shape,
        compiler_params=_params(("parallel",), blocks, 2 * _nbytes((tr, D_MODEL), F32)),
        name=name,
    )(*args)
    return tuple(out) if emit_x else out[0]


def _mm_body(*refs, kind, scale):
    if kind == "rotary":
        a_ref, w_ref, cos_ref, sin_ref, o_ref = refs
    elif kind == "resid":
        a_ref, w_ref, x_ref, ga_ref, o_ref = refs
    else:
        a_ref, w_ref, o_ref = refs
    acc = jnp.dot(a_ref[...].astype(BF16), w_ref[...].astype(BF16), preferred_element_type=F32)
    if kind == "gelu":
        o_ref[...] = jax.nn.gelu(acc, approximate=True).astype(o_ref.dtype)
    elif kind == "plain":
        o_ref[...] = acc.astype(o_ref.dtype)
    elif kind == "resid":
        o_ref[...] = x_ref[...] + ga_ref[...] * acc
    else:
        cos, sin = cos_ref[...], sin_ref[...]
        half = RET_DK // 2
        for h0 in range(0, acc.shape[1], RET_DK):
            t1 = acc[:, h0:h0 + half]
            t2 = acc[:, h0 + half:h0 + RET_DK]
            o_ref[:, h0:h0 + half] = ((t1 * cos - t2 * sin) * scale).astype(o_ref.dtype)
            o_ref[:, h0 + half:h0 + RET_DK] = ((t2 * cos + t1 * sin) * scale).astype(o_ref.dtype)


def _matmul(path, a, w, layer, *, col0, n_out, kind, out_dtype, cos=None, sin=None, scale=1.0,
            x=None, mod=None, mod_layer=0, ga_which=None, name):
    rows, k_dim = a.shape
    tm, tn = min(rows, 1024), 512
    joff = col0 // tn
    args = [a, w]
    specs = [pl.BlockSpec((tm, k_dim), lambda i, j: (i, 0)),
             pl.BlockSpec((None, k_dim, tn), lambda i, j: (layer, 0, j + joff))]
    if kind == "rotary":
        args += [cos, sin]
        specs += [path.pos_spec(tm), path.pos_spec(tm)]
    if kind == "resid":
        args += [x, mod]
        specs += [pl.BlockSpec((tm, tn), lambda i, j: (i, j)),
                  path.mod_spec(tm, mod_layer, ga_which, width=tn)]
    blocks = (_nbytes((tm, k_dim), a.dtype) + _nbytes((k_dim, tn), w.dtype)
              + (2 + 2 * path.per_row) * _nbytes((tm, tn), out_dtype))
    temps = _nbytes((k_dim, tn), BF16) + 2 * _nbytes((tm, tn), F32)
    return pl.pallas_call(
        functools.partial(_mm_body, kind=kind, scale=scale),
        grid=(rows // tm, n_out // tn),
        in_specs=specs,
        out_specs=pl.BlockSpec((tm, tn), lambda i, j: (i, j)),
        out_shape=jax.ShapeDtypeStruct((rows, n_out), out_dtype),
        compiler_params=_params(("parallel", "parallel"), blocks, temps),
        name=name,
    )(*args)


def _mm_kt_body(a_ref, w_ref, o_ref):
    @pl.when(pl.program_id(1) == 0)
    def _():
        o_ref[...] = jnp.zeros_like(o_ref)

    o_ref[...] += jnp.dot(a_ref[...].astype(BF16), w_ref[...].astype(BF16),
                          preferred_element_type=F32)


def _matmul_ktiled(a, w, layer, *, name):
    rows, k_dim = a.shape
    n_out = w.shape[-1]
    tm, tk = 512, 512
    blocks = _nbytes((tm, tk), a.dtype) + _nbytes((tk, n_out), w.dtype) + _nbytes((tm, n_out), F32)
    temps = _nbytes((tk, n_out), BF16) + _nbytes((tm, n_out), F32)
    return pl.pallas_call(
        _mm_kt_body,
        grid=(rows // tm, k_dim // tk),
        in_specs=[pl.BlockSpec((tm, tk), lambda i, k: (i, k)),
                  pl.BlockSpec((None, tk, n_out), lambda i, k: (layer, k, 0))],
        out_specs=pl.BlockSpec((tm, n_out), lambda i, k: (i, 0)),
        out_shape=jax.ShapeDtypeStruct((rows, n_out), F32),
        compiler_params=_params(("parallel", "arbitrary"), blocks, temps),
        name=name,
    )(a, w)


def _vnorm_body(v_ref, g_ref, o_ref):
    v = v_ref[...]
    o_ref[...] = (v * lax.rsqrt(jnp.mean(v * v, axis=-1, keepdims=True) + EPS)) * g_ref[...]


def _vnorm(z, v_gain, layer, *, name):
    rows = z.shape[0]
    tr = 256
    blk = _nbytes((tr, GM_WIDTH), F32)
    return pl.pallas_call(
        _vnorm_body,
        grid=(rows // tr,),
        in_specs=[pl.BlockSpec((tr, GM_WIDTH), lambda i: (i, 1)),
                  pl.BlockSpec((None, 1, GM_WIDTH), lambda i: (layer, 0, 0))],
        out_specs=pl.BlockSpec((tr, GM_WIDTH), lambda i: (i, 0)),
        out_shape=jax.ShapeDtypeStruct((rows, GM_WIDTH), F32),
        compiler_params=_params(("parallel",), 2 * blk, 2 * blk),
        name=name,
    )(z, v_gain)


def _gate_body(u_ref, vn_ref, w_ref, b_ref, o_ref, *, span):
    w, bias = w_ref[...], b_ref[...]
    for r0 in range(0, u_ref.shape[0], span):
        rows = slice(r0, r0 + span)
        sv = jnp.dot(w, vn_ref[rows, :].astype(BF16), preferred_element_type=F32) + bias
        o_ref[rows, :] = (u_ref[rows, :] * sv).astype(o_ref.dtype)


def _gate(z, vn, w_mix, bias, *, name):
    rows = z.shape[0]
    span = w_mix.shape[1]
    rb = 1024
    gd = GM_GROUP_DIM
    blocks = (2 * _nbytes((rb, gd), F32) + _nbytes((span, span), BF16)
              + _nbytes((span, LANES), F32) + _nbytes((rb, gd), BF16))
    return pl.pallas_call(
        functools.partial(_gate_body, span=span),
        grid=(rows // rb, GM_GROUPS),
        in_specs=[pl.BlockSpec((rb, gd), lambda c, g: (c, g)),
                  pl.BlockSpec((rb, gd), lambda c, g: (c, g)),
                  pl.BlockSpec((None, span, span), lambda c, g: (g, 0, 0)),
                  pl.BlockSpec((None, span, 1), lambda c, g: (g, 0, 0))],
        out_specs=pl.BlockSpec((rb, gd), lambda c, g: (c, g)),
        out_shape=jax.ShapeDtypeStruct((rows, GM_WIDTH), BF16),
        compiler_params=_params(("parallel", "parallel"), blocks, 4 * _nbytes((rb, gd), F32)),
        name=name,
    )(z, vn, w_mix, bias)


def _gate_grouped_body(u_ref, vn_ref, w_ref, b_ref, o_ref):
    vn = vn_ref[...].astype(BF16).astype(F32)
    w = w_ref[...].astype(BF16).astype(F32)
    sv = w[0][None] * vn[:, 0:1, :]
    for j in range(1, w.shape[0]):
        sv = sv + w[j][None] * vn[:, j:j + 1, :]
    o_ref[...] = u_ref[...] * (sv + b_ref[...][None])


def _gate_grouped(z, vn, w_mix_t, bias, *, name):
    n_seq, lc, _ = vn.shape
    gd = GM_GROUP_DIM
    blk = pl.BlockSpec((n_seq, lc, gd), lambda g: (0, 0, g))
    blocks = 3 * _nbytes((n_seq, lc, gd), F32) + (lc + 1) * _nbytes((lc, LANES), F32)
    return pl.pallas_call(
        _gate_grouped_body,
        grid=(GM_GROUPS,),
        in_specs=[blk, blk,
                  pl.BlockSpec((None, lc, lc, 1), lambda g: (g, 0, 0, 0)),
                  pl.BlockSpec((None, lc, 1), lambda g: (g, 0, 0))],
        out_specs=blk,
        out_shape=jax.ShapeDtypeStruct((n_seq, lc, GM_WIDTH), F32),
        compiler_params=_params(("parallel",), blocks, 4 * _nbytes((n_seq, lc, gd), F32)),
        name=name,
    )(z, vn, w_mix_t, bias)


def _ret_tables(lc):
    lg = jnp.log1p(-jnp.exp2(-5.0 - jnp.arange(RET_HEADS, dtype=F32)))
    idx = jnp.arange(lc, dtype=F32)
    diff = idx[:, None] - idx[None, :]
    decay = jnp.where(diff >= 0, jnp.exp(jnp.maximum(diff, 0.0)[None] * lg[:, None, None]), 0.0)
    o_scale = jnp.exp((idx + 1.0)[None, :] * lg[:, None])[..., None]
    k_scale = jnp.exp((lc - 1.0 - idx)[None, :] * lg[:, None])[..., None]
    s_scale = jnp.broadcast_to(jnp.exp(lc * lg)[:, None, None], (RET_HEADS, 1, RET_DV))
    return decay, o_scale, k_scale, s_scale


def _ret_chunk(q, k, v, g, s, decay, o_scale, k_scale, s_scale, gain):
    qb, vb = q.astype(BF16), v.astype(BF16)
    scores = lax.dot_general(qb, k.astype(BF16), (((1,), (1,)), ((), ())),
                             preferred_element_type=F32) * decay
    o = jnp.dot(scores.astype(BF16), vb, preferred_element_type=F32)
    o = o + jnp.dot(qb, s.astype(BF16), preferred_element_type=F32) * o_scale
    k_w = (k * k_scale).astype(BF16)
    s_new = s_scale * s + lax.dot_general(k_w, vb, (((0,), (0,)), ((), ())),
                                          preferred_element_type=F32)
    mu = jnp.mean(o, axis=-1, keepdims=True)
    var = jnp.mean(jnp.square(o - mu), axis=-1, keepdims=True)
    o = ((o - mu) * lax.rsqrt(var + EPS)) * gain
    return o * jax.nn.silu(g), s_new


def _state_out(n_layers, shape, s_all):
    out_shape = jax.ShapeDtypeStruct((n_layers,) + shape, F32)
    if s_all is None:
        return out_shape, [], []
    return out_shape, [s_all], [pl.BlockSpec(memory_space=pl.ANY)]


def _ret_seq_body(q_ref, k_ref, v_ref, g_ref, gain_ref, dec_ref, osc_ref, ksc_ref, ssc_ref, s0_ref,
                  *rest):
    o_ref, s_ref = rest[-2:]
    s_ref[...] = s0_ref[...]
    decay, o_scale, k_scale = dec_ref[...], osc_ref[...], ksc_ref[...]
    s_scale, gain = ssc_ref[...], gain_ref[...]

    def chunk(c, carry):
        rows = pl.ds(pl.multiple_of(c * CHUNK, CHUNK), CHUNK)
        out, s_new = _ret_chunk(q_ref[rows, :], k_ref[rows, :], v_ref[rows, :], g_ref[rows, :],
                                s_ref[...], decay, o_scale, k_scale, s_scale, gain)
        s_ref[...] = s_new
        o_ref[rows, :] = out.astype(o_ref.dtype)
        return carry

    lax.fori_loop(0, q_ref.shape[0] // CHUNK, chunk, 0)


def _retention_seq(q, k, v, g, gn_gain, layer, s0, s_all, *, name):
    n_seq, seq_len, _ = q.shape
    decay, o_scale, k_scale, s_scale = _ret_tables(CHUNK)
    qk_spec = pl.BlockSpec((None, seq_len, RET_DK), lambda b, h: (b, 0, h))
    v_spec = pl.BlockSpec((None, seq_len, RET_DV), lambda b, h: (b, 0, h))
    s_spec = pl.BlockSpec((None, None, None, RET_DK, RET_DV), lambda b, h: (layer, b, h, 0, 0))
    tab = lambda shape: pl.BlockSpec((None,) + shape, lambda b, h: (h, 0, 0))
    blocks = (_nbytes((seq_len, RET_DK), q.dtype) + _nbytes((seq_len, RET_DK), k.dtype)
              + _nbytes((seq_len, RET_DV), v.dtype) + _nbytes((seq_len, RET_DV), g.dtype)
              + _nbytes((seq_len, RET_DV), BF16) + 2 * _nbytes((RET_DK, RET_DV), F32))
    s_shape, extra_args, extra_specs = _state_out(s0.shape[0], s0.shape[1:], s_all)
    args = [q, k, v, g, gn_gain, decay, o_scale, k_scale, s_scale, s0]
    return pl.pallas_call(
        _ret_seq_body,
        grid=(n_seq, RET_HEADS),
        in_specs=[qk_spec, qk_spec, v_spec, v_spec,
                  pl.BlockSpec((None, 1, RET_DV), lambda b, h: (layer, 0, h)),
                  tab((CHUNK, CHUNK)), tab((CHUNK, 1)), tab((CHUNK, 1)), tab((1, RET_DV)), s_spec]
        + extra_specs,
        out_specs=[v_spec, s_spec],
        out_shape=[jax.ShapeDtypeStruct((n_seq, seq_len, RET_V), BF16), s_shape],
        input_output_aliases={len(args): 1} if extra_args else {},
        compiler_params=_params(("parallel", "parallel"), blocks, 8 * MIB),
        name=name,
    )(*args, *extra_args)


def _ret_step_body(q_ref, k_ref, v_ref, g_ref, gain_ref, dec_ref, osc_ref, ksc_ref, ssc_ref, s0_ref,
                   *rest):
    o_ref, s_ref = rest[-2:]
    for h in range(RET_HEADS):
        qk = slice(h * RET_DK, (h + 1) * RET_DK)
        vv = slice(h * RET_DV, (h + 1) * RET_DV)
        out, s_new = _ret_chunk(q_ref[:, qk], k_ref[:, qk], v_ref[:, vv], g_ref[:, vv], s0_ref[h],
                                dec_ref[h], osc_ref[h], ksc_ref[h], ssc_ref[h], gain_ref[:, vv])
        s_ref[h] = s_new
        o_ref[:, vv] = out.astype(o_ref.dtype)


def _retention_step(q, k, v, g, gn_gain, layer, s0, s_all, *, name):
    n_seq, seq_len, _ = q.shape
    decay, o_scale, k_scale, s_scale = _ret_tables(seq_len)
    qk_spec = pl.BlockSpec((None, seq_len, RET_QK), lambda b: (b, 0, 0))
    v_spec = pl.BlockSpec((None, seq_len, RET_V), lambda b: (b, 0, 0))
    s_spec = pl.BlockSpec((None, None, RET_HEADS, RET_DK, RET_DV), lambda b: (layer, b, 0, 0, 0))
    whole = lambda a: pl.BlockSpec(a.shape, lambda b: (0,) * a.ndim)
    blocks = 2 * _nbytes((RET_HEADS, RET_DK, RET_DV), F32) + 8 * _nbytes((seq_len, RET_V), F32)
    s_shape, extra_args, extra_specs = _state_out(s0.shape[0], s0.shape[1:], s_all)
    args = [q, k, v, g, gn_gain, decay, o_scale, k_scale, s_scale, s0]
    return pl.pallas_call(
        _ret_step_body,
        grid=(n_seq,),
        in_specs=[qk_spec, qk_spec, v_spec, v_spec,
                  pl.BlockSpec((None, 1, RET_V), lambda b: (layer, 0, 0)),
                  whole(decay), whole(o_scale), whole(k_scale), whole(s_scale), s_spec]
        + extra_specs,
        out_specs=[v_spec, s_spec],
        out_shape=[jax.ShapeDtypeStruct((n_seq, seq_len, RET_V), F32), s_shape],
        input_output_aliases={len(args): 1} if extra_args else {},
        compiler_params=_params(("parallel",), blocks, 8 * MIB),
        name=name,
    )(*args, *extra_args)


def _ffn_body(*refs, grouped, tiles_per_seq, fuse_resid, cast_next):
    it = iter(refs)
    h_ref, wa_ref, wb_ref, wd_ref, cw_ref, cb_ref, st_ref = (next(it) for _ in range(7))
    if fuse_resid:
        x_hbm, ga_ref = next(it), next(it)
    if cast_next:
        nu_ref, nd_ref = next(it), next(it)
    o_ref, tail_ref = next(it), next(it)
    if cast_next:
        nu_out_ref, nd_out_ref = next(it), next(it)
    if not grouped:
        carry_ref, prev_ref = next(it), next(it)
    i, j = pl.program_id(0), pl.program_id(1)
    tm = h_ref.shape[0]

    @pl.when(j == 0)
    def _():
        if fuse_resid:
            pltpu.sync_copy(x_hbm.at[pl.ds(pl.multiple_of(i * tm, tm), tm), :], o_ref)
        else:
            o_ref[...] = jnp.zeros_like(o_ref)

    if cast_next:
        nu_out_ref[...] = nu_ref[...].astype(BF16)
        nd_out_ref[...] = nd_ref[...].astype(BF16)

    h = h_ref[...]
    a = jnp.dot(h, wa_ref[...].astype(BF16), preferred_element_type=F32)
    b = jnp.dot(h, wb_ref[...].astype(BF16), preferred_element_type=F32)

    if grouped:
        pos = lax.broadcasted_iota(jnp.int32, a.shape, 0) & (SUBLANES - 1)
        st = st_ref[...]
        a_m1 = jnp.where(pos == 0, pltpu.roll(st, tm - 1, 0), pltpu.roll(a, 1, 0))
        a_m2 = jnp.where(pos < 2, st, pltpu.roll(a, 2, 0))
        tail_ref[...] = a
    else:
        @pl.when(i % tiles_per_seq == 0)
        def _():
            prev_ref[...] = st_ref[...]

        @pl.when(i % tiles_per_seq != 0)
        def _():
            prev_ref[...] = carry_ref[j]

        a_ext = jnp.concatenate([prev_ref[...], a], axis=0)
        a_m1 = pltpu.roll(a_ext, 1, 0)[SUBLANES:]
        a_m2 = pltpu.roll(a_ext, 2, 0)[SUBLANES:]
        last = a[tm - SUBLANES:, :]
        carry_ref[j] = last
        tail_ref[...] = last

    conv = cb_ref[...] + a_m2 * cw_ref[0:1, :] + a_m1 * cw_ref[1:2, :] + a * cw_ref[2:3, :]
    gated = (jax.nn.silu(conv) * b).astype(BF16)
    w_down = wd_ref[...].astype(BF16)
    for c0 in range(0, D_MODEL, FFN_DOWN_CHUNK):
        cols = slice(c0, c0 + FFN_DOWN_CHUNK)
        p = jnp.dot(gated, w_down[:, cols], preferred_element_type=F32)
        if fuse_resid:
            p = ga_ref[:, cols] * p
        o_ref[:, cols] += p


def _conv_ffn(path, h, w_up, conv_w, conv_b, w_down, layer, state, *, x=None, mod=None,
              mod_layer=0, next_w=None, name):
    rows = path.rows
    tm = FFN_TM
    n_tiles, nj = rows // tm, D_FF // FFN_TF
    grouped = path.per_row
    fuse_resid, cast_next = x is not None, next_w is not None
    if grouped:
        tiles_per_seq = 0
        st_spec = pl.BlockSpec((None, tm, FFN_TF), lambda i, j: (layer, i, j))
        tail_shape = jax.ShapeDtypeStruct((rows, D_FF), F32)
        tail_spec = pl.BlockSpec((tm, FFN_TF), lambda i, j: (i, j))
        scratch = []
    else:
        tiles_per_seq = path.seq_len // tm
        st_spec = pl.BlockSpec((None, None, SUBLANES, FFN_TF),
                               lambda i, j: (layer, i // tiles_per_seq, 0, j))
        tail_shape = jax.ShapeDtypeStruct((n_tiles, SUBLANES, D_FF), F32)
        tail_spec = pl.BlockSpec((None, SUBLANES, FFN_TF), lambda i, j: (i, 0, j))
        scratch = [pltpu.VMEM((nj, SUBLANES, FFN_TF), F32), pltpu.VMEM((SUBLANES, FFN_TF), F32)]
    once = dict(pipeline_mode=pl.Buffered(1))
    single = _nbytes((tm, D_MODEL), BF16) + _nbytes((tm, D_MODEL), F32)
    blocks = (2 * _nbytes((D_MODEL, FFN_TF), w_up.dtype) + _nbytes((FFN_TF, D_MODEL), w_down.dtype)
              + 2 * _nbytes((tm, FFN_TF), F32))
    temps = 10 * _nbytes((tm, FFN_TF), F32) + _nbytes((tm, FFN_DOWN_CHUNK), F32)

    args = [h, w_up, w_up, w_down, conv_w, conv_b.reshape(DEPTH, 1, D_FF), state]
    in_specs = [pl.BlockSpec((tm, D_MODEL), lambda i, j: (i, 0), **once),
                pl.BlockSpec((None, D_MODEL, FFN_TF), lambda i, j: (0, 0, j)),
                pl.BlockSpec((None, D_MODEL, FFN_TF), lambda i, j: (0, 0, j + nj)),
                pl.BlockSpec((None, FFN_TF, D_MODEL), lambda i, j: (0, j, 0)),
                pl.BlockSpec((None, CONV_WIDTH, FFN_TF), lambda i, j: (layer, 0, j)),
                pl.BlockSpec((None, 1, FFN_TF), lambda i, j: (layer, 0, j)),
                st_spec]
    out_shape = [jax.ShapeDtypeStruct((rows, D_MODEL), F32), tail_shape]
    out_specs = [pl.BlockSpec((tm, D_MODEL), lambda i, j: (i, 0), **once), tail_spec]
    if fuse_resid:
        args += [x, mod]
        in_specs += [pl.BlockSpec(memory_space=pl.ANY), path.mod_spec(tm, mod_layer, 5)]
    if cast_next:
        up_all, down_all, nxt = next_w
        up_blk = (D_MODEL // n_tiles, 2 * D_FF // nj)
        down_blk = (D_FF // nj, D_MODEL // n_tiles)
        args += [up_all, down_all]
        in_specs += [pl.BlockSpec((None,) + up_blk, lambda i, j: (nxt, i, j)),
                     pl.BlockSpec((None,) + down_blk, lambda i, j: (nxt, j, i))]
        out_shape += [jax.ShapeDtypeStruct((1, D_MODEL, 2 * D_FF), BF16),
                      jax.ShapeDtypeStruct((1, D_FF, D_MODEL), BF16)]
        out_specs += [pl.BlockSpec((None,) + up_blk, lambda i, j: (0, i, j)),
                      pl.BlockSpec((None,) + down_blk, lambda i, j: (0, j, i))]
        blocks += 3 * (_nbytes(up_blk, BF16) + _nbytes(down_blk, BF16))
    res = pl.pallas_call(
        functools.partial(_ffn_body, grouped=grouped, tiles_per_seq=tiles_per_seq,
                          fuse_resid=fuse_resid, cast_next=cast_next),
        grid=(n_tiles, nj),
        in_specs=in_specs,
        out_specs=out_specs,
        out_shape=out_shape,
        scratch_shapes=scratch,
        compiler_params=_params(("arbitrary", "arbitrary"), blocks, temps, single),
        name=name,
    )(*args)
    out, tail = res[0], res[1]
    if not grouped:
        tail = tail[tiles_per_seq - 1::tiles_per_seq]
    return out, tail, (tuple(res[2:]) if cast_next else None)


def _rotary_tables(pos):
    half = RET_DK // 2
    inv_freq = ROPE_BASE ** (-jnp.arange(half, dtype=F32) / half)
    ang = pos.astype(F32)[:, None] * inv_freq[None, :]
    return jnp.cos(ang), jnp.sin(ang)


def _decoder(path, tag, x, mod, pos, ret_state, conv_state, g_norm_mix, g_norm_ffn, g_norm_final,
             gm_w_in, gm_v_gain, gm_w_s, gm_b_s, gm_w_out, ret_w_in, ret_gn_gain, ret_w_out,
             ffn_w_up, ffn_conv_w, ffn_conv_b, ffn_w_down, ffn_bf16=None):
    n_seq, seq_len = path.n_seq, path.seq_len
    make_bf16 = ffn_bf16 is None
    if make_bf16:
        ffn_bf16 = [(ffn_w_up[:1].astype(BF16), ffn_w_down[:1].astype(BF16))] + [None] * (DEPTH - 1)
    fuse_ffn_resid = not path.per_row
    lc = min(seq_len, CHUNK)
    seq_dtype = F32 if path.per_row else BF16

    cos, sin = _rotary_tables(pos)
    if path.per_row:
        cos, sin = jnp.tile(cos, (n_seq, 1)), jnp.tile(sin, (n_seq, 1))

    tril = jnp.tril(gm_w_s[:, :, :lc, :lc])
    bias = gm_b_s[:, :, :lc, None]
    if path.per_row:
        w_mix = jnp.swapaxes(tril, 2, 3)[..., None]
    else:
        w_mix = tril.astype(BF16)

    x = x.reshape(path.rows, D_MODEL)
    h = _resid_norm(path, x, g_norm_mix, 0, mod=mod, sc_which=1, sh_which=0, name=f"{tag}_norm_in")
    new_v, new_conv = [], []
    s_all = None
    y_out = None
    for i in range(DEPTH):
        j = i // N_MIXERS
        if i % N_MIXERS == 0:
            z = _matmul(path, h, gm_w_in, j, col0=0, n_out=2 * GM_WIDTH, kind="gelu", out_dtype=F32,
                        name=f"{tag}_gm_in{i}")
            vn = _vnorm(z, gm_v_gain, j, name=f"{tag}_gm_vnorm{i}")
            new_v.append(vn)
            if path.per_row:
                gated = _gate_grouped(z.reshape(n_seq, seq_len, -1), vn.reshape(n_seq, seq_len, -1),
                                      w_mix[j], bias[j], name=f"{tag}_gm_gate{i}")
                gated = gated.reshape(path.rows, GM_WIDTH)
            else:
                gated = _gate(z, vn, w_mix[j], bias[j], name=f"{tag}_gm_gate{i}")
            x = _matmul(path, gated, gm_w_out, j, col0=0, n_out=D_MODEL, kind="resid", out_dtype=F32,
                        x=x, mod=mod, mod_layer=i, ga_which=2, name=f"{tag}_gm_out{i}")
            y = None
        else:
            proj = functools.partial(_matmul, path, h, ret_w_in, j)
            q = proj(col0=0, n_out=RET_QK, kind="rotary", out_dtype=seq_dtype, cos=cos, sin=sin,
                     name=f"{tag}_ret_q{i}")
            k = proj(col0=RET_QK, n_out=RET_QK, kind="rotary", out_dtype=F32, cos=cos, sin=sin,
                     scale=RET_DK ** -0.5, name=f"{tag}_ret_k{i}")
            v = proj(col0=2 * RET_QK, n_out=RET_V, kind="plain", out_dtype=seq_dtype,
                     name=f"{tag}_ret_v{i}")
            g = proj(col0=2 * RET_QK + RET_V, n_out=RET_V, kind="plain", out_dtype=F32,
                     name=f"{tag}_ret_g{i}")
            shp = lambda t: t.reshape(n_seq, seq_len, t.shape[-1])
            ret = _retention_step if path.per_row else _retention_seq
            o, s_all = ret(shp(q), shp(k), shp(v), shp(g), ret_gn_gain, j, ret_state, s_all,
                           name=f"{tag}_retention{i}")
            o = o.reshape(path.rows, RET_V)
            if path.per_row:
                y = _matmul_ktiled(o, ret_w_out, j, name=f"{tag}_ret_out{i}")
            else:
                x = _matmul(path, o, ret_w_out, j, col0=0, n_out=D_MODEL, kind="resid",
                            out_dtype=F32, x=x, mod=mod, mod_layer=i, ga_which=2,
                            name=f"{tag}_ret_out{i}")
                y = None

        if y is None:
            h = _resid_norm(path, x, g_norm_ffn, i, mod=mod, sc_which=4, sh_which=3,
                            name=f"{tag}_norm_ffn{i}")
        else:
            x, h = _resid_norm(path, x, g_norm_ffn, i, mod=mod, mod_layer=i, y=y, ga_which=2,
                               sc_which=4, sh_which=3, emit_x=True, name=f"{tag}_norm_ffn{i}")
        next_w = (ffn_w_up, ffn_w_down, i + 1) if make_bf16 and i + 1 < DEPTH else None
        w_up_i, w_down_i = ffn_bf16[i]
        y, tail, cast = _conv_ffn(path, h, w_up_i, ffn_conv_w, ffn_conv_b, w_down_i, i, conv_state,
                                  x=x if fuse_ffn_resid else None, mod=mod, mod_layer=i,
                                  next_w=next_w, name=f"{tag}_ffn{i}")
        if cast is not None:
            ffn_bf16[i + 1] = cast
        if path.per_row:
            tail = tail.reshape(n_seq, seq_len, D_FF)
        new_conv.append(tail[:, -(CONV_WIDTH - 1):, :])
        if fuse_ffn_resid:
            x, resid = y, {}
        else:
            resid = dict(mod_layer=i, y=y, ga_which=5)
        if i + 1 < DEPTH:
            out = _resid_norm(path, x, g_norm_mix, i + 1, mod=mod, sc_which=1, sh_which=0,
                              emit_x=not fuse_ffn_resid, name=f"{tag}_norm_mix{i + 1}", **resid)
            x, h = (x, out) if fuse_ffn_resid else out
        else:
            y_out = _resid_norm(path, x, g_norm_final, 0, mod=mod, out_dtype=F32,
                                name=f"{tag}_norm_out", **resid)
    y_out = y_out.reshape(n_seq, seq_len, D_MODEL)
    new_v = [t.reshape(n_seq, seq_len, GM_WIDTH) for t in new_v]
    return y_out, s_all, jnp.stack(new_v), jnp.stack(new_conv), ffn_bf16


def kernel(x_prompt, x_sample, c_prompt, c_sample, state_ret, state_conv, w_mod, b_mod, g_norm_mix, g_norm_ffn, g_norm_final, gm_w_in, gm_v_gain, gm_w_s, gm_b_s, gm_w_out, ret_w_in, ret_gn_gain, ret_w_out, ffn_w_up, ffn_conv_w, ffn_conv_b, ffn_w_down):
    n_p, len_p, _ = x_prompt.shape
    n_s, len_s, _ = x_sample.shape
    prompt, sample = _Path(n_p, len_p), _Path(n_s, len_s)

    n_c = n_p + n_s
    pad = -n_c % (2 * SUBLANES)
    c_all = jnp.concatenate([c_prompt, c_sample, jnp.zeros((pad, D_MODEL), F32)], axis=0)
    mod_all = _modulation(c_all, w_mod, b_mod)
    mod_p = mod_all[:, :n_p].reshape(DEPTH, n_p, 1, 6 * D_MODEL)
    mod_s = jnp.repeat(mod_all[:, n_p:n_c], len_s, axis=1)

    weights = (g_norm_mix.reshape(DEPTH, 1, D_MODEL), g_norm_ffn.reshape(DEPTH, 1, D_MODEL),
               g_norm_final.reshape(1, 1, D_MODEL), gm_w_in, gm_v_gain.reshape(-1, 1, GM_WIDTH),
               gm_w_s, gm_b_s, gm_w_out, ret_w_in, ret_gn_gain.reshape(-1, 1, RET_V),
               ret_w_out.astype(BF16), ffn_w_up, ffn_conv_w, ffn_conv_b, ffn_w_down)

    n_ret = state_ret.shape[0]
    hist = CONV_WIDTH - 1
    ret0 = jnp.zeros((n_ret, n_p, RET_HEADS, RET_DK, RET_DV), F32)
    conv_p = jnp.zeros((DEPTH, n_p, SUBLANES, D_FF), F32)
    conv_s = jnp.pad(state_conv, ((0, 0), (0, 0), (0, len_s - hist), (0, 0))).reshape(
        DEPTH, n_s * len_s, D_FF)

    y_p, ret_p, _, cv_p, ffn_bf16 = _decoder(prompt, "p", x_prompt, mod_p, jnp.arange(len_p), ret0,
                                             conv_p, *weights)
    y_s, ret_s, v_s, cv_s, _ = _decoder(sample, "s", x_sample, mod_s, PAST_LEN + jnp.arange(len_s),
                                        state_ret, conv_s, *weights, ffn_bf16=ffn_bf16)
    return (y_p, y_s, ret_p, ret_s, v_s, cv_p, cv_s)
```

```python
import functools

import jax
import jax.numpy as jnp
from jax import lax
from jax.experimental import pallas as pl
from jax.experimental.pallas import tpu as pltpu

D_MODEL = 4096
DEPTH = 4
PAST_LEN = 16384
N_MIXERS = 2
CHUNK = 128
GM_WIDTH = D_MODEL
GM_GROUPS = 16
GM_GROUP_DIM = GM_WIDTH // GM_GROUPS
RET_HEADS = 16
RET_DK = D_MODEL // RET_HEADS
RET_DV = 2 * RET_DK
RET_QK = RET_HEADS * RET_DK
RET_V = RET_HEADS * RET_DV
ROPE_BASE = 10000.0
D_FF = 11008
CONV_WIDTH = 3
EPS = 1e-6

F32 = jnp.float32
BF16 = jnp.bfloat16

SUBLANES = 8
LANES = 128
MIB = 1024 * 1024
VMEM_CAP_MIB = 60
FFN_TF = 256
FFN_TM = 1024
FFN_DOWN_CHUNK = 1024
RET_HEADS_PER_STEP = 2


def _nbytes(shape, dtype):
    n = jnp.dtype(dtype).itemsize
    for s in shape:
        n *= s
    return n


def _params(semantics, pipelined_bytes, temp_bytes=0, single_bytes=0):
    need = 2 * pipelined_bytes + single_bytes + temp_bytes
    mib = min(VMEM_CAP_MIB, -(-need // MIB) + 4)
    return pltpu.CompilerParams(dimension_semantics=semantics, vmem_limit_bytes=mib * MIB)


class _Path:
    def __init__(self, n_seq, seq_len):
        self.n_seq = n_seq
        self.seq_len = seq_len
        self.rows = n_seq * seq_len
        self.per_row = seq_len < SUBLANES * 2

    def mod_spec(self, tile_rows, layer, which, width=D_MODEL):
        per_vec = D_MODEL // width
        col = (lambda rest: which * per_vec + rest[0]) if per_vec > 1 else (lambda rest: which)
        if self.per_row:
            return pl.BlockSpec((None, tile_rows, width), lambda i, *rest: (layer, i, col(rest)))
        tiles_per_seq = self.seq_len // tile_rows
        return pl.BlockSpec((None, None, 1, width),
                            lambda i, *rest: (layer, i // tiles_per_seq, 0, col(rest)))

    def pos_spec(self, tile_rows):
        if self.per_row:
            return pl.BlockSpec((tile_rows, LANES), lambda i, *_: (i, 0))
        tiles_per_seq = self.seq_len // tile_rows
        return pl.BlockSpec((tile_rows, LANES), lambda i, *_: (i % tiles_per_seq, 0))


def _mod_body(c_ref, w_ref, b_ref, o_ref):
    c_act = jax.nn.silu(c_ref[...]).astype(BF16)
    o_ref[...] = jnp.dot(c_act, w_ref[...].astype(BF16), preferred_element_type=F32) + b_ref[...]


def _modulation(c_all, w_mod, b_mod):
    rows = c_all.shape[0]
    n_out = 6 * D_MODEL
    tn = 512
    blocks = _nbytes((rows, D_MODEL), F32) + _nbytes((D_MODEL, tn), F32) + _nbytes((rows, tn), F32)
    return pl.pallas_call(
        _mod_body,
        grid=(DEPTH, n_out // tn),
        in_specs=[pl.BlockSpec((rows, D_MODEL), lambda l, j: (0, 0)),
                  pl.BlockSpec((None, D_MODEL, tn), lambda l, j: (l, 0, j)),
                  pl.BlockSpec((None, 1, tn), lambda l, j: (l, 0, j))],
        out_specs=pl.BlockSpec((None, rows, tn), lambda l, j: (l, 0, j)),
        out_shape=jax.ShapeDtypeStruct((DEPTH, rows, n_out), F32),
        compiler_params=_params(("parallel", "parallel"), blocks, _nbytes((D_MODEL, tn), BF16)),
        name="modulation",
    )(c_all, w_mod, b_mod.reshape(DEPTH, 1, n_out))


def _resid_norm_body(*refs, has_resid, has_mod, emit_x):
    it = iter(refs)
    x_ref = next(it)
    if has_resid:
        y_ref, ga_ref = next(it), next(it)
    gain_ref = next(it)
    if has_mod:
        sc_ref, sh_ref = next(it), next(it)
    if emit_x:
        xo_ref = next(it)
    h_ref = next(it)

    x = x_ref[...]
    if has_resid:
        x = x + ga_ref[...] * y_ref[...]
    if emit_x:
        xo_ref[...] = x
    n = (x * lax.rsqrt(jnp.mean(x * x, axis=-1, keepdims=True) + EPS)) * gain_ref[...]
    if has_mod:
        n = n * (1.0 + sc_ref[...]) + sh_ref[...]
    h_ref[...] = n.astype(h_ref.dtype)


def _resid_norm(path, x, gain, layer, *, mod=None, mod_layer=0, y=None, ga_which=None,
                sc_which=None, sh_which=None, emit_x=False, out_dtype=BF16, name):
    tr = 256
    rows = path.rows
    row_spec = pl.BlockSpec((tr, D_MODEL), lambda i: (i, 0))
    has_resid, has_mod = y is not None, sc_which is not None
    args, specs = [x], [row_spec]
    if has_resid:
        args += [y, mod]
        specs += [row_spec, path.mod_spec(tr, mod_layer, ga_which)]
    args.append(gain)
    specs.append(pl.BlockSpec((None, 1, D_MODEL), lambda i: (layer, 0, 0)))
    if has_mod:
        args += [mod, mod]
        specs += [path.mod_spec(tr, layer, sc_which), path.mod_spec(tr, layer, sh_which)]
    out_shape, out_specs = [], []
    if emit_x:
        out_shape.append(jax.ShapeDtypeStruct((rows, D_MODEL), F32))
        out_specs.append(row_spec)
    out_shape.append(jax.ShapeDtypeStruct((rows, D_MODEL), out_dtype))
    out_specs.append(row_spec)
    blocks = _nbytes((tr, D_MODEL), F32) * (4 + 3 * path.per_row)
    out = pl.pallas_call(
        functools.partial(_resid_norm_body, has_resid=has_resid, has_mod=has_mod, emit_x=emit_x),
        grid=(rows // tr,),
        in_specs=specs,
        out_specs=out_specs,
        out_shape=out_shape,
        compiler_params=_params(("parallel",), blocks, 2 * _nbytes((tr, D_MODEL), F32)),
        name=name,
    )(*args)
    return tuple(out) if emit_x else out[0]


def _mm_body(*refs, kind, scale):
    if kind == "rotary":
        a_ref, w_ref, cos_ref, sin_ref, o_ref = refs
    elif kind == "resid":
        a_ref, w_ref, x_ref, ga_ref, o_ref = refs
    else:
        a_ref, w_ref, o_ref = refs
    acc = jnp.dot(a_ref[...].astype(BF16), w_ref[...].astype(BF16), preferred_element_type=F32)
    if kind == "gelu":
        o_ref[...] = jax.nn.gelu(acc, approximate=True).astype(o_ref.dtype)
    elif kind == "plain":
        o_ref[...] = acc.astype(o_ref.dtype)
    elif kind == "resid":
        o_ref[...] = x_ref[...] + ga_ref[...] * acc
    else:
        cos, sin = cos_ref[...], sin_ref[...]
        half = RET_DK // 2
        for h0 in range(0, acc.shape[1], RET_DK):
            t1 = acc[:, h0:h0 + half]
            t2 = acc[:, h0 + half:h0 + RET_DK]
            o_ref[:, h0:h0 + half] = ((t1 * cos - t2 * sin) * scale).astype(o_ref.dtype)
            o_ref[:, h0 + half:h0 + RET_DK] = ((t2 * cos + t1 * sin) * scale).astype(o_ref.dtype)


def _matmul(path, a, w, layer, *, col0, n_out, kind, out_dtype, cos=None, sin=None, scale=1.0,
            x=None, mod=None, mod_layer=0, ga_which=None, name):
    rows, k_dim = a.shape
    tm, tn = min(rows, 1024), 512
    joff = col0 // tn
    args = [a, w]
    specs = [pl.BlockSpec((tm, k_dim), lambda i, j: (i, 0)),
             pl.BlockSpec((None, k_dim, tn), lambda i, j: (layer, 0, j + joff))]
    if kind == "rotary":
        args += [cos, sin]
        specs += [path.pos_spec(tm), path.pos_spec(tm)]
    if kind == "resid":
        args += [x, mod]
        specs += [pl.BlockSpec((tm, tn), lambda i, j: (i, j)),
                  path.mod_spec(tm, mod_layer, ga_which, width=tn)]
    blocks = (_nbytes((tm, k_dim), a.dtype) + _nbytes((k_dim, tn), w.dtype)
              + (2 + 2 * path.per_row) * _nbytes((tm, tn), out_dtype))
    temps = _nbytes((k_dim, tn), BF16) + 2 * _nbytes((tm, tn), F32)
    return pl.pallas_call(
        functools.partial(_mm_body, kind=kind, scale=scale),
        grid=(rows // tm, n_out // tn),
        in_specs=specs,
        out_specs=pl.BlockSpec((tm, tn), lambda i, j: (i, j)),
        out_shape=jax.ShapeDtypeStruct((rows, n_out), out_dtype),
        compiler_params=_params(("parallel", "parallel"), blocks, temps),
        name=name,
    )(*args)


def _mm_kt_body(a_ref, w_ref, o_ref):
    @pl.when(pl.program_id(1) == 0)
    def _():
        o_ref[...] = jnp.zeros_like(o_ref)

    o_ref[...] += jnp.dot(a_ref[...].astype(BF16), w_ref[...].astype(BF16),
                          preferred_element_type=F32)


def _matmul_ktiled(a, w, layer, *, name):
    rows, k_dim = a.shape
    n_out = w.shape[-1]
    tm, tk = 512, 512
    blocks = _nbytes((tm, tk), a.dtype) + _nbytes((tk, n_out), w.dtype) + _nbytes((tm, n_out), F32)
    temps = _nbytes((tk, n_out), BF16) + _nbytes((tm, n_out), F32)
    return pl.pallas_call(
        _mm_kt_body,
        grid=(rows // tm, k_dim // tk),
        in_specs=[pl.BlockSpec((tm, tk), lambda i, k: (i, k)),
                  pl.BlockSpec((None, tk, n_out), lambda i, k: (layer, k, 0))],
        out_specs=pl.BlockSpec((tm, n_out), lambda i, k: (i, 0)),
        out_shape=jax.ShapeDtypeStruct((rows, n_out), F32),
        compiler_params=_params(("parallel", "arbitrary"), blocks, temps),
        name=name,
    )(a, w)


def _vnorm_body(v_ref, g_ref, o_ref):
    v = v_ref[...]
    o_ref[...] = (v * lax.rsqrt(jnp.mean(v * v, axis=-1, keepdims=True) + EPS)) * g_ref[...]


def _vnorm(z, v_gain, layer, *, name):
    rows = z.shape[0]
    tr = 256
    blk = _nbytes((tr, GM_WIDTH), F32)
    return pl.pallas_call(
        _vnorm_body,
        grid=(rows // tr,),
        in_specs=[pl.BlockSpec((tr, GM_WIDTH), lambda i: (i, 1)),
                  pl.BlockSpec((None, 1, GM_WIDTH), lambda i: (layer, 0, 0))],
        out_specs=pl.BlockSpec((tr, GM_WIDTH), lambda i: (i, 0)),
        out_shape=jax.ShapeDtypeStruct((rows, GM_WIDTH), F32),
        compiler_params=_params(("parallel",), 2 * blk, 2 * blk),
        name=name,
    )(z, v_gain)


def _gate_body(u_ref, vn_ref, w_ref, b_ref, o_ref, *, span):
    w, bias = w_ref[...], b_ref[...]
    for r0 in range(0, u_ref.shape[0], span):
        rows = slice(r0, r0 + span)
        sv = jnp.dot(w, vn_ref[rows, :].astype(BF16), preferred_element_type=F32) + bias
        o_ref[rows, :] = (u_ref[rows, :] * sv).astype(o_ref.dtype)


def _gate(z, vn, w_mix, bias, *, name):
    rows = z.shape[0]
    span = w_mix.shape[1]
    rb = 1024
    gd = GM_GROUP_DIM
    blocks = (2 * _nbytes((rb, gd), F32) + _nbytes((span, span), BF16)
              + _nbytes((span, LANES), F32) + _nbytes((rb, gd), BF16))
    return pl.pallas_call(
        functools.partial(_gate_body, span=span),
        grid=(rows // rb, GM_GROUPS),
        in_specs=[pl.BlockSpec((rb, gd), lambda c, g: (c, g)),
                  pl.BlockSpec((rb, gd), lambda c, g: (c, g)),
                  pl.BlockSpec((None, span, span), lambda c, g: (g, 0, 0)),
                  pl.BlockSpec((None, span, 1), lambda c, g: (g, 0, 0))],
        out_specs=pl.BlockSpec((rb, gd), lambda c, g: (c, g)),
        out_shape=jax.ShapeDtypeStruct((rows, GM_WIDTH), BF16),
        compiler_params=_params(("parallel", "parallel"), blocks, 4 * _nbytes((rb, gd), F32)),
        name=name,
    )(z, vn, w_mix, bias)


def _gate_grouped_body(u_ref, vn_ref, w_ref, b_ref, o_ref):
    vn = vn_ref[...].astype(BF16).astype(F32)
    w = w_ref[...].astype(BF16).astype(F32)
    sv = w[0][None] * vn[:, 0:1, :]
    for j in range(1, w.shape[0]):
        sv = sv + w[j][None] * vn[:, j:j + 1, :]
    o_ref[...] = u_ref[...] * (sv + b_ref[...][None])


def _gate_grouped(z, vn, w_mix_t, bias, *, name):
    n_seq, lc, _ = vn.shape
    gd = GM_GROUP_DIM
    blk = pl.BlockSpec((n_seq, lc, gd), lambda g: (0, 0, g))
    blocks = 3 * _nbytes((n_seq, lc, gd), F32) + (lc + 1) * _nbytes((lc, LANES), F32)
    return pl.pallas_call(
        _gate_grouped_body,
        grid=(GM_GROUPS,),
        in_specs=[blk, blk,
                  pl.BlockSpec((None, lc, lc, 1), lambda g: (g, 0, 0, 0)),
                  pl.BlockSpec((None, lc, 1), lambda g: (g, 0, 0))],
        out_specs=blk,
        out_shape=jax.ShapeDtypeStruct((n_seq, lc, GM_WIDTH), F32),
        compiler_params=_params(("parallel",), blocks, 4 * _nbytes((n_seq, lc, gd), F32)),
        name=name,
    )(z, vn, w_mix_t, bias)


def _ret_tables(lc):
    lg = jnp.log1p(-jnp.exp2(-5.0 - jnp.arange(RET_HEADS, dtype=F32)))
    idx = jnp.arange(lc, dtype=F32)
    diff = idx[:, None] - idx[None, :]
    decay = jnp.where(diff >= 0, jnp.exp(jnp.maximum(diff, 0.0)[None] * lg[:, None, None]), 0.0)
    o_scale = jnp.exp((idx + 1.0)[None, :] * lg[:, None])[..., None]
    k_scale = jnp.exp((lc - 1.0 - idx)[None, :] * lg[:, None])[..., None]
    s_scale = jnp.broadcast_to(jnp.exp(lc * lg)[:, None, None], (RET_HEADS, 1, RET_DV))
    return decay, o_scale, k_scale, s_scale


def _ret_chunk(q, k, v, g, s, decay, o_scale, k_scale, s_scale, gain):
    qb, vb = q.astype(BF16), v.astype(BF16)
    scores = lax.dot_general(qb, k.astype(BF16), (((1,), (1,)), ((), ())),
                             preferred_element_type=F32) * decay
    o = jnp.dot(scores.astype(BF16), vb, preferred_element_type=F32)
    o = o + jnp.dot(qb, s.astype(BF16), preferred_element_type=F32) * o_scale
    k_w = (k * k_scale).astype(BF16)
    s_new = s_scale * s + lax.dot_general(k_w, vb, (((0,), (0,)), ((), ())),
                                          preferred_element_type=F32)
    mu = jnp.mean(o, axis=-1, keepdims=True)
    var = jnp.mean(jnp.square(o - mu), axis=-1, keepdims=True)
    o = ((o - mu) * lax.rsqrt(var + EPS)) * gain
    return o * jax.nn.silu(g), s_new


def _state_out(n_layers, shape, s_all):
    out_shape = jax.ShapeDtypeStruct((n_layers,) + shape, F32)
    if s_all is None:
        return out_shape, [], []
    return out_shape, [s_all], [pl.BlockSpec(memory_space=pl.ANY)]


def _ret_seq_body(q_ref, k_ref, v_ref, g_ref, gain_ref, dec_ref, osc_ref, ksc_ref, ssc_ref, s0_ref,
                  *rest):
    o_ref, s_ref = rest[-2:]
    s_ref[...] = s0_ref[...]
    n_heads = s_ref.shape[0]

    def chunk(c, carry):
        rows = pl.ds(pl.multiple_of(c * CHUNK, CHUNK), CHUNK)
        for h in range(n_heads):
            qk = slice(h * RET_DK, (h + 1) * RET_DK)
            vv = slice(h * RET_DV, (h + 1) * RET_DV)
            out, s_new = _ret_chunk(q_ref[rows, qk], k_ref[rows, qk], v_ref[rows, vv],
                                    g_ref[rows, vv], s_ref[h], dec_ref[h], osc_ref[h], ksc_ref[h],
                                    ssc_ref[h], gain_ref[:, vv])
            s_ref[h] = s_new
            o_ref[rows, vv] = out.astype(o_ref.dtype)
        return carry

    lax.fori_loop(0, q_ref.shape[0] // CHUNK, chunk, 0)


def _retention_seq(q, k, v, g, gn_gain, layer, s0, s_all, *, name):
    n_seq, seq_len, _ = q.shape
    hp = RET_HEADS_PER_STEP
    decay, o_scale, k_scale, s_scale = _ret_tables(CHUNK)
    qk_spec = pl.BlockSpec((None, seq_len, hp * RET_DK), lambda b, h: (b, 0, h))
    v_spec = pl.BlockSpec((None, seq_len, hp * RET_DV), lambda b, h: (b, 0, h))
    s_spec = pl.BlockSpec((None, None, hp, RET_DK, RET_DV), lambda b, h: (layer, b, h, 0, 0))
    tab = lambda shape: pl.BlockSpec((hp,) + shape, lambda b, h: (h, 0, 0))
    blocks = hp * (_nbytes((seq_len, RET_DK), q.dtype) + _nbytes((seq_len, RET_DK), k.dtype)
                   + _nbytes((seq_len, RET_DV), v.dtype) + _nbytes((seq_len, RET_DV), g.dtype)
                   + _nbytes((seq_len, RET_DV), BF16) + 2 * _nbytes((RET_DK, RET_DV), F32))
    s_shape, extra_args, extra_specs = _state_out(s0.shape[0], s0.shape[1:], s_all)
    args = [q, k, v, g, gn_gain, decay, o_scale, k_scale, s_scale, s0]
    return pl.pallas_call(
        _ret_seq_body,
        grid=(n_seq, RET_HEADS // hp),
        in_specs=[qk_spec, qk_spec, v_spec, v_spec,
                  pl.BlockSpec((None, 1, hp * RET_DV), lambda b, h: (layer, 0, h)),
                  tab((CHUNK, CHUNK)), tab((CHUNK, 1)), tab((CHUNK, 1)), tab((1, RET_DV)), s_spec]
        + extra_specs,
        out_specs=[v_spec, s_spec],
        out_shape=[jax.ShapeDtypeStruct((n_seq, seq_len, RET_V), BF16), s_shape],
        input_output_aliases={len(args): 1} if extra_args else {},
        compiler_params=_params(("parallel", "parallel"), blocks, 8 * MIB),
        name=name,
    )(*args, *extra_args)


def _ret_step_body(q_ref, k_ref, v_ref, g_ref, gain_ref, dec_ref, osc_ref, ksc_ref, ssc_ref, s0_ref,
                   *rest):
    o_ref, s_ref = rest[-2:]
    for h in range(RET_HEADS):
        qk = slice(h * RET_DK, (h + 1) * RET_DK)
        vv = slice(h * RET_DV, (h + 1) * RET_DV)
        out, s_new = _ret_chunk(q_ref[:, qk], k_ref[:, qk], v_ref[:, vv], g_ref[:, vv], s0_ref[h],
                                dec_ref[h], osc_ref[h], ksc_ref[h], ssc_ref[h], gain_ref[:, vv])
        s_ref[h] = s_new
        o_ref[:, vv] = out.astype(o_ref.dtype)


def _retention_step(q, k, v, g, gn_gain, layer, s0, s_all, *, name):
    n_seq, seq_len, _ = q.shape
    decay, o_scale, k_scale, s_scale = _ret_tables(seq_len)
    qk_spec = pl.BlockSpec((None, seq_len, RET_QK), lambda b: (b, 0, 0))
    v_spec = pl.BlockSpec((None, seq_len, RET_V), lambda b: (b, 0, 0))
    s_spec = pl.BlockSpec((None, None, RET_HEADS, RET_DK, RET_DV), lambda b: (layer, b, 0, 0, 0))
    whole = lambda a: pl.BlockSpec(a.shape, lambda b: (0,) * a.ndim)
    blocks = 2 * _nbytes((RET_HEADS, RET_DK, RET_DV), F32) + 8 * _nbytes((seq_len, RET_V), F32)
    s_shape, extra_args, extra_specs = _state_out(s0.shape[0], s0.shape[1:], s_all)
    args = [q, k, v, g, gn_gain, decay, o_scale, k_scale, s_scale, s0]
    return pl.pallas_call(
        _ret_step_body,
        grid=(n_seq,),
        in_specs=[qk_spec, qk_spec, v_spec, v_spec,
                  pl.BlockSpec((None, 1, RET_V), lambda b: (layer, 0, 0)),
                  whole(decay), whole(o_scale), whole(k_scale), whole(s_scale), s_spec]
        + extra_specs,
        out_specs=[v_spec, s_spec],
        out_shape=[jax.ShapeDtypeStruct((n_seq, seq_len, RET_V), F32), s_shape],
        input_output_aliases={len(args): 1} if extra_args else {},
        compiler_params=_params(("parallel",), blocks, 8 * MIB),
        name=name,
    )(*args, *extra_args)


def _ffn_body(*refs, grouped, tiles_per_seq, fuse_resid, cast_next):
    it = iter(refs)
    h_ref, wa_ref, wb_ref, wd_ref, cw_ref, cb_ref, st_ref = (next(it) for _ in range(7))
    if fuse_resid:
        x_hbm, ga_ref = next(it), next(it)
    if cast_next:
        nu_ref, nd_ref = next(it), next(it)
    o_ref, tail_ref = next(it), next(it)
    if cast_next:
        nu_out_ref, nd_out_ref = next(it), next(it)
    if not grouped:
        carry_ref, prev_ref = next(it), next(it)
    i, j = pl.program_id(0), pl.program_id(1)
    tm = h_ref.shape[0]

    @pl.when(j == 0)
    def _():
        if fuse_resid:
            pltpu.sync_copy(x_hbm.at[pl.ds(pl.multiple_of(i * tm, tm), tm), :], o_ref)
        else:
            o_ref[...] = jnp.zeros_like(o_ref)

    if cast_next:
        nu_out_ref[...] = nu_ref[...].astype(BF16)
        nd_out_ref[...] = nd_ref[...].astype(BF16)

    h = h_ref[...]
    a = jnp.dot(h, wa_ref[...].astype(BF16), preferred_element_type=F32)
    b = jnp.dot(h, wb_ref[...].astype(BF16), preferred_element_type=F32)

    if grouped:
        pos = lax.broadcasted_iota(jnp.int32, a.shape, 0) & (SUBLANES - 1)
        st = st_ref[...]
        a_m1 = jnp.where(pos == 0, pltpu.roll(st, tm - 1, 0), pltpu.roll(a, 1, 0))
        a_m2 = jnp.where(pos < 2, st, pltpu.roll(a, 2, 0))
        tail_ref[...] = a
    else:
        @pl.when(i % tiles_per_seq == 0)
        def _():
            prev_ref[...] = st_ref[...]

        @pl.when(i % tiles_per_seq != 0)
        def _():
            prev_ref[...] = carry_ref[j]

        a_ext = jnp.concatenate([prev_ref[...], a], axis=0)
        a_m1 = pltpu.roll(a_ext, 1, 0)[SUBLANES:]
        a_m2 = pltpu.roll(a_ext, 2, 0)[SUBLANES:]
        last = a[tm - SUBLANES:, :]
        carry_ref[j] = last
        tail_ref[...] = last

    conv = cb_ref[...] + a_m2 * cw_ref[0:1, :] + a_m1 * cw_ref[1:2, :] + a * cw_ref[2:3, :]
    gated = (jax.nn.silu(conv) * b).astype(BF16)
    w_down = wd_ref[...].astype(BF16)
    for c0 in range(0, D_MODEL, FFN_DOWN_CHUNK):
        cols = slice(c0, c0 + FFN_DOWN_CHUNK)
        p = jnp.dot(gated, w_down[:, cols], preferred_element_type=F32)
        if fuse_resid:
            p = ga_ref[:, cols] * p
        o_ref[:, cols] += p


def _conv_ffn(path, h, w_up, conv_w, conv_b, w_down, layer, state, *, x=None, mod=None,
              mod_layer=0, next_w=None, name):
    rows = path.rows
    tm = FFN_TM
    n_tiles, nj = rows // tm, D_FF // FFN_TF
    grouped = path.per_row
    fuse_resid, cast_next = x is not None, next_w is not None
    if grouped:
        tiles_per_seq = 0
        st_spec = pl.BlockSpec((None, tm, FFN_TF), lambda i, j: (layer, i, j))
        tail_shape = jax.ShapeDtypeStruct((rows, D_FF), F32)
        tail_spec = pl.BlockSpec((tm, FFN_TF), lambda i, j: (i, j))
        scratch = []
    else:
        tiles_per_seq = path.seq_len // tm
        st_spec = pl.BlockSpec((None, None, SUBLANES, FFN_TF),
                               lambda i, j: (layer, i // tiles_per_seq, 0, j))
        tail_shape = jax.ShapeDtypeStruct((n_tiles, SUBLANES, D_FF), F32)
        tail_spec = pl.BlockSpec((None, SUBLANES, FFN_TF), lambda i, j: (i, 0, j))
        scratch = [pltpu.VMEM((nj, SUBLANES, FFN_TF), F32), pltpu.VMEM((SUBLANES, FFN_TF), F32)]
    once = dict(pipeline_mode=pl.Buffered(1))
    single = _nbytes((tm, D_MODEL), BF16) + _nbytes((tm, D_MODEL), F32)
    blocks = (2 * _nbytes((D_MODEL, FFN_TF), w_up.dtype) + _nbytes((FFN_TF, D_MODEL), w_down.dtype)
              + 2 * _nbytes((tm, FFN_TF), F32))
    temps = 10 * _nbytes((tm, FFN_TF), F32) + _nbytes((tm, FFN_DOWN_CHUNK), F32)

    args = [h, w_up, w_up, w_down, conv_w, conv_b.reshape(DEPTH, 1, D_FF), state]
    in_specs = [pl.BlockSpec((tm, D_MODEL), lambda i, j: (i, 0), **once),
                pl.BlockSpec((None, D_MODEL, FFN_TF), lambda i, j: (0, 0, j)),
                pl.BlockSpec((None, D_MODEL, FFN_TF), lambda i, j: (0, 0, j + nj)),
                pl.BlockSpec((None, FFN_TF, D_MODEL), lambda i, j: (0, j, 0)),
                pl.BlockSpec((None, CONV_WIDTH, FFN_TF), lambda i, j: (layer, 0, j)),
                pl.BlockSpec((None, 1, FFN_TF), lambda i, j: (layer, 0, j)),
                st_spec]
    out_shape = [jax.ShapeDtypeStruct((rows, D_MODEL), F32), tail_shape]
    out_specs = [pl.BlockSpec((tm, D_MODEL), lambda i, j: (i, 0), **once), tail_spec]
    if fuse_resid:
        args += [x, mod]
        in_specs += [pl.BlockSpec(memory_space=pl.ANY), path.mod_spec(tm, mod_layer, 5)]
    if cast_next:
        up_all, down_all, nxt = next_w
        up_blk = (D_MODEL // n_tiles, 2 * D_FF // nj)
        down_blk = (D_FF // nj, D_MODEL // n_tiles)
        args += [up_all, down_all]
        in_specs += [pl.BlockSpec((None,) + up_blk, lambda i, j: (nxt, i, j)),
                     pl.BlockSpec((None,) + down_blk, lambda i, j: (nxt, j, i))]
        out_shape += [jax.ShapeDtypeStruct((1, D_MODEL, 2 * D_FF), BF16),
                      jax.ShapeDtypeStruct((1, D_FF, D_MODEL), BF16)]
        out_specs += [pl.BlockSpec((None,) + up_blk, lambda i, j: (0, i, j)),
                      pl.BlockSpec((None,) + down_blk, lambda i, j: (0, j, i))]
        blocks += 3 * (_nbytes(up_blk, BF16) + _nbytes(down_blk, BF16))
    res = pl.pallas_call(
        functools.partial(_ffn_body, grouped=grouped, tiles_per_seq=tiles_per_seq,
                          fuse_resid=fuse_resid, cast_next=cast_next),
        grid=(n_tiles, nj),
        in_specs=in_specs,
        out_specs=out_specs,
        out_shape=out_shape,
        scratch_shapes=scratch,
        compiler_params=_params(("arbitrary", "arbitrary"), blocks, temps, single),
        name=name,
    )(*args)
    out, tail = res[0], res[1]
    if not grouped:
        tail = tail[tiles_per_seq - 1::tiles_per_seq]
    return out, tail, (tuple(res[2:]) if cast_next else None)


def _rotary_tables(pos):
    half = RET_DK // 2
    inv_freq = ROPE_BASE ** (-jnp.arange(half, dtype=F32) / half)
    ang = pos.astype(F32)[:, None] * inv_freq[None, :]
    return jnp.cos(ang), jnp.sin(ang)


def _decoder(path, tag, x, mod, pos, ret_state, conv_state, g_norm_mix, g_norm_ffn, g_norm_final,
             gm_w_in, gm_v_gain, gm_w_s, gm_b_s, gm_w_out, ret_w_in, ret_gn_gain, ret_w_out,
             ffn_w_up, ffn_conv_w, ffn_conv_b, ffn_w_down, ffn_bf16=None):
    n_seq, seq_len = path.n_seq, path.seq_len
    make_bf16 = ffn_bf16 is None
    if make_bf16:
        ffn_bf16 = [(ffn_w_up[:1].astype(BF16), ffn_w_down[:1].astype(BF16))] + [None] * (DEPTH - 1)
    fuse_ffn_resid = not path.per_row
    lc = min(seq_len, CHUNK)
    seq_dtype = F32 if path.per_row else BF16

    cos, sin = _rotary_tables(pos)
    if path.per_row:
        cos, sin = jnp.tile(cos, (n_seq, 1)), jnp.tile(sin, (n_seq, 1))

    tril = jnp.tril(gm_w_s[:, :, :lc, :lc])
    bias = gm_b_s[:, :, :lc, None]
    if path.per_row:
        w_mix = jnp.swapaxes(tril, 2, 3)[..., None]
    else:
        w_mix = tril.astype(BF16)

    x = x.reshape(path.rows, D_MODEL)
    h = _resid_norm(path, x, g_norm_mix, 0, mod=mod, sc_which=1, sh_which=0, name=f"{tag}_norm_in")
    new_v, new_conv = [], []
    s_all = None
    y_out = None
    for i in range(DEPTH):
        j = i // N_MIXERS
        if i % N_MIXERS == 0:
            z = _matmul(path, h, gm_w_in, j, col0=0, n_out=2 * GM_WIDTH, kind="gelu", out_dtype=F32,
                        name=f"{tag}_gm_in{i}")
            vn = _vnorm(z, gm_v_gain, j, name=f"{tag}_gm_vnorm{i}")
            new_v.append(vn)
            if path.per_row:
                gated = _gate_grouped(z.reshape(n_seq, seq_len, -1), vn.reshape(n_seq, seq_len, -1),
                                      w_mix[j], bias[j], name=f"{tag}_gm_gate{i}")
                gated = gated.reshape(path.rows, GM_WIDTH)
            else:
                gated = _gate(z, vn, w_mix[j], bias[j], name=f"{tag}_gm_gate{i}")
            x = _matmul(path, gated, gm_w_out, j, col0=0, n_out=D_MODEL, kind="resid", out_dtype=F32,
                        x=x, mod=mod, mod_layer=i, ga_which=2, name=f"{tag}_gm_out{i}")
            y = None
        else:
            proj = functools.partial(_matmul, path, h, ret_w_in, j)
            q = proj(col0=0, n_out=RET_QK, kind="rotary", out_dtype=seq_dtype, cos=cos, sin=sin,
                     name=f"{tag}_ret_q{i}")
            k = proj(col0=RET_QK, n_out=RET_QK, kind="rotary", out_dtype=F32, cos=cos, sin=sin,
                     scale=RET_DK ** -0.5, name=f"{tag}_ret_k{i}")
            v = proj(col0=2 * RET_QK, n_out=RET_V, kind="plain", out_dtype=seq_dtype,
                     name=f"{tag}_ret_v{i}")
            g = proj(col0=2 * RET_QK + RET_V, n_out=RET_V, kind="plain", out_dtype=F32,
                     name=f"{tag}_ret_g{i}")
            shp = lambda t: t.reshape(n_seq, seq_len, t.shape[-1])
            ret = _retention_step if path.per_row else _retention_seq
            o, s_all = ret(shp(q), shp(k), shp(v), shp(g), ret_gn_gain, j, ret_state, s_all,
                           name=f"{tag}_retention{i}")
            o = o.reshape(path.rows, RET_V)
            if path.per_row:
                y = _matmul_ktiled(o, ret_w_out, j, name=f"{tag}_ret_out{i}")
            else:
                x = _matmul(path, o, ret_w_out, j, col0=0, n_out=D_MODEL, kind="resid",
                            out_dtype=F32, x=x, mod=mod, mod_layer=i, ga_which=2,
                            name=f"{tag}_ret_out{i}")
                y = None

        if y is None:
            h = _resid_norm(path, x, g_norm_ffn, i, mod=mod, sc_which=4, sh_which=3,
                            name=f"{tag}_norm_ffn{i}")
        else:
            x, h = _resid_norm(path, x, g_norm_ffn, i, mod=mod, mod_layer=i, y=y, ga_which=2,
                               sc_which=4, sh_which=3, emit_x=True, name=f"{tag}_norm_ffn{i}")
        next_w = (ffn_w_up, ffn_w_down, i + 1) if make_bf16 and i + 1 < DEPTH else None
        w_up_i, w_down_i = ffn_bf16[i]
        y, tail, cast = _conv_ffn(path, h, w_up_i, ffn_conv_w, ffn_conv_b, w_down_i, i, conv_state,
                                  x=x if fuse_ffn_resid else None, mod=mod, mod_layer=i,
                                  next_w=next_w, name=f"{tag}_ffn{i}")
        if cast is not None:
            ffn_bf16[i + 1] = cast
        if path.per_row:
            tail = tail.reshape(n_seq, seq_len, D_FF)
        new_conv.append(tail[:, -(CONV_WIDTH - 1):, :])
        if fuse_ffn_resid:
            x, resid = y, {}
        else:
            resid = dict(mod_layer=i, y=y, ga_which=5)
        if i + 1 < DEPTH:
            out = _resid_norm(path, x, g_norm_mix, i + 1, mod=mod, sc_which=1, sh_which=0,
                              emit_x=not fuse_ffn_resid, name=f"{tag}_norm_mix{i + 1}", **resid)
            x, h = (x, out) if fuse_ffn_resid else out
        else:
            y_out = _resid_norm(path, x, g_norm_final, 0, mod=mod, out_dtype=F32,
                                name=f"{tag}_norm_out", **resid)
    y_out = y_out.reshape(n_seq, seq_len, D_MODEL)
    new_v = [t.reshape(n_seq, seq_len, GM_WIDTH) for t in new_v]
    return y_out, s_all, jnp.stack(new_v), jnp.stack(new_conv), ffn_bf16


def kernel(x_prompt, x_sample, c_prompt, c_sample, state_ret, state_conv, w_mod, b_mod, g_norm_mix, g_norm_ffn, g_norm_final, gm_w_in, gm_v_gain, gm_w_s, gm_b_s, gm_w_out, ret_w_in, ret_gn_gain, ret_w_out, ffn_w_up, ffn_conv_w, ffn_conv_b, ffn_w_down):
    n_p, len_p, _ = x_prompt.shape
    n_s, len_s, _ = x_sample.shape
    prompt, sample = _Path(n_p, len_p), _Path(n_s, len_s)

    n_c = n_p + n_s
    pad = -n_c % (2 * SUBLANES)
    c_all = jnp.concatenate([c_prompt, c_sample, jnp.zeros((pad, D_MODEL), F32)], axis=0)
    mod_all = _modulation(c_all, w_mod, b_mod)
    mod_p = mod_all[:, :n_p].reshape(DEPTH, n_p, 1, 6 * D_MODEL)
    mod_s = jnp.repeat(mod_all[:, n_p:n_c], len_s, axis=1)

    weights = (g_norm_mix.reshape(DEPTH, 1, D_MODEL), g_norm_ffn.reshape(DEPTH, 1, D_MODEL),
               g_norm_final.reshape(1, 1, D_MODEL), gm_w_in, gm_v_gain.reshape(-1, 1, GM_WIDTH),
               gm_w_s, gm_b_s, gm_w_out, ret_w_in, ret_gn_gain.reshape(-1, 1, RET_V),
               ret_w_out.astype(BF16), ffn_w_up, ffn_conv_w, ffn_conv_b, ffn_w_down)

    n_ret = state_ret.shape[0]
    hist = CONV_WIDTH - 1
    ret0 = jnp.zeros((n_ret, n_p, RET_HEADS, RET_DK, RET_DV), F32)
    conv_p = jnp.zeros((DEPTH, n_p, SUBLANES, D_FF), F32)
    conv_s = jnp.pad(state_conv, ((0, 0), (0, 0), (0, len_s - hist), (0, 0))).reshape(
        DEPTH, n_s * len_s, D_FF)

    y_p, ret_p, _, cv_p, ffn_bf16 = _decoder(prompt, "p", x_prompt, mod_p, jnp.arange(len_p), ret0,
                                             conv_p, *weights)
    y_s, ret_s, v_s, cv_s, _ = _decoder(sample, "s", x_sample, mod_s, PAST_LEN + jnp.arange(len_s),
                                        state_ret, conv_s, *weights, ffn_bf16=ffn_bf16)
    return (y_p, y_s, ret_p, ret_s, v_s, cv_p, cv_s)
```
